```python
import math
import jax, jax.numpy as jnp
from jax import lax
import numpy as np

D_MODEL = 2048
BATCH = 1
SEQ = 8192
DEPTH = 4

CHUNK = 64
N_A_LAYERS = DEPTH // 2
N_B_LAYERS = DEPTH - N_A_LAYERS
S5_GROUP = 16
S5_GROUPS = D_MODEL // S5_GROUP
S5_STATE = 64
DT_MIN = 1e-3
DT_MAX = 1e-1
FOX_HEAD_DIM = 128
FOX_HEADS = D_MODEL // FOX_HEAD_DIM
Q_BLOCK = 128
FGATE_BIAS_LO = 1.0
FGATE_BIAS_HI = 6.0
D_FF = 5632
CONV_WIDTH = 3
NORM_EPS = 1e-6

kernel_name = "s5_fox_yoco_convglu_trunk"


def rms_norm(x, g):
    xf = x.astype(jnp.float32)
    y = xf * lax.rsqrt(jnp.mean(xf * xf, axis=-1, keepdims=True) + NORM_EPS)
    return (y * g.astype(jnp.float32)).astype(x.dtype)


def s5_mixer(h, w_in, lam_re, lam_im, log_step, b_re, b_im, c_re, c_im, d, w_glu):
    f32 = jnp.float32
    bsz, seq, _ = h.shape
    u = (h @ w_in).astype(f32).reshape(bsz, seq, S5_GROUPS, S5_GROUP)
    lam = lax.complex(lam_re.astype(f32), lam_im.astype(f32))
    dt = jnp.exp(log_step.astype(f32))[:, None]
    lam_bar = jnp.exp(lam * dt)
    b = lax.complex(b_re.astype(f32), b_im.astype(f32))
    b_bar = ((lam_bar - 1.0) / lam)[..., None] * b
    bu = jnp.einsum('blgc,gpc->blgp', u.astype(jnp.complex64), b_bar)
    a = jnp.broadcast_to(lam_bar, bu.shape)

    def combine(e1, e2):
        a1, s1 = e1
        a2, s2 = e2
        return a1 * a2, a2 * s1 + s2

    _, states = lax.associative_scan(combine, (a, bu), axis=1)
    c = lax.complex(c_re.astype(f32), c_im.astype(f32))
    y = jnp.real(jnp.einsum('blgp,gcp->blgc', states, c)) + d.astype(f32) * u
    y = jax.nn.gelu(y.reshape(bsz, seq, D_MODEL)).astype(h.dtype)
    val, gate = jnp.split(y @ w_glu, 2, axis=-1)
    return val * jax.nn.sigmoid(gate)


def fox_attention(h, w_q, w_o, k, v, logf_cum):
    bsz, seq, _ = h.shape
    n_blk = seq // Q_BLOCK
    q = (h @ w_q).reshape(bsz, n_blk, Q_BLOCK, FOX_HEADS, FOX_HEAD_DIM).transpose(1, 0, 2, 3, 4)
    cq = logf_cum.reshape(bsz, FOX_HEADS, n_blk, Q_BLOCK).transpose(2, 0, 1, 3)
    k_pos = jnp.arange(seq)
    scale = FOX_HEAD_DIM ** -0.5

    def block(args):
        qb, cqb, blk = args
        q_pos = blk * Q_BLOCK + jnp.arange(Q_BLOCK)
        s = jnp.einsum('bqhd,bkhd->bhqk', qb, k).astype(jnp.float32) * scale
        s = s + (cqb[..., :, None] - logf_cum[:, :, None, :])
        s = jnp.where(k_pos[None, :] <= q_pos[:, None], s, -jnp.inf)
        p = jax.nn.softmax(s, axis=-1)
        return jnp.einsum('bhqk,bkhd->bqhd', p.astype(v.dtype), v)

    o = lax.map(block, (q, cq, jnp.arange(n_blk)))
    o = o.transpose(1, 0, 2, 3, 4).reshape(bsz, seq, D_MODEL)
    return o @ w_o


def conv_glu_ffn(h, w_up, conv_w, conv_b, w_down):
    gate, val = jnp.split(h @ w_up, 2, axis=-1)
    gate = lax.conv_general_dilated(
        gate, conv_w[:, None, :], window_strides=(1,),
        padding=[(CONV_WIDTH - 1, 0)],
        dimension_numbers=('NWC', 'WIO', 'NWC'),
        feature_group_count=D_FF) + conv_b
    return (jax.nn.gelu(gate) * val) @ w_down


def setup_inputs(seed: int = 0) -> dict:
    key = jax.random.key(seed)
    ks = jax.random.split(key, 32)
    f32 = jnp.float32

    def nrm(k, shape, scale):
        return jax.random.normal(k, shape, f32) * scale

    na, nb, G, P, C, H = N_A_LAYERS, N_B_LAYERS, S5_GROUPS, S5_STATE, S5_GROUP, FOX_HEADS
    n_idx = jnp.arange(P, dtype=f32)
    return {
        "x": nrm(ks[0], (BATCH, SEQ, D_MODEL), 1.0),
        "a_norm": 1.0 + nrm(ks[1], (na, D_MODEL), 0.02),
        "a_w_in": nrm(ks[2], (na, D_MODEL, D_MODEL), D_MODEL ** -0.5),
        "a_lambda_re": -0.5 + nrm(ks[3], (na, G, P), 0.01),
        "a_lambda_im": math.pi * n_idx + nrm(ks[4], (na, G, P), 0.01),
        "a_log_step": jax.random.uniform(ks[5], (na, G), f32, math.log(DT_MIN), math.log(DT_MAX)),
        "a_b_re": nrm(ks[6], (na, G, P, C), (2 * C) ** -0.5),
        "a_b_im": nrm(ks[7], (na, G, P, C), (2 * C) ** -0.5),
        "a_c_re": nrm(ks[8], (na, G, C, P), (2 * P) ** -0.5),
        "a_c_im": nrm(ks[9], (na, G, C, P), (2 * P) ** -0.5),
        "a_d": nrm(ks[10], (na, G, C), 0.5),
        "a_w_glu": nrm(ks[11], (na, D_MODEL, 2 * D_MODEL), D_MODEL ** -0.5),
        "kv_norm": 1.0 + nrm(ks[12], (D_MODEL,), 0.02),
        "w_k": nrm(ks[13], (D_MODEL, D_MODEL), D_MODEL ** -0.5),
        "w_v": nrm(ks[14], (D_MODEL, D_MODEL), D_MODEL ** -0.5),
        "w_f": nrm(ks[15], (D_MODEL, H), D_MODEL ** -0.5),
        "b_f": jnp.linspace(FGATE_BIAS_LO, FGATE_BIAS_HI, H, dtype=f32) + nrm(ks[16], (H,), 0.1),
        "b_norm": 1.0 + nrm(ks[17], (nb, D_MODEL), 0.02),
        "b_w_q": nrm(ks[18], (nb, D_MODEL, D_MODEL), D_MODEL ** -0.5),
        "b_w_o": nrm(ks[19], (nb, D_MODEL, D_MODEL), D_MODEL ** -0.5),
        "ffn_norm": 1.0 + nrm(ks[20], (DEPTH, D_MODEL), 0.02),
        "ffn_w_up": nrm(ks[21], (DEPTH, D_MODEL, 2 * D_FF), D_MODEL ** -0.5),
        "ffn_conv_w": nrm(ks[22], (DEPTH, CONV_WIDTH, D_FF), CONV_WIDTH ** -0.5),
        "ffn_conv_b": nrm(ks[23], (DEPTH, D_FF), 0.01),
        "ffn_w_down": nrm(ks[24], (DEPTH, D_FF, D_MODEL), D_FF ** -0.5),
        "final_norm": 1.0 + nrm(ks[25], (D_MODEL,), 0.02),
    }


def reference(x, a_norm, a_w_in, a_lambda_re, a_lambda_im, a_log_step, a_b_re, a_b_im,
              a_c_re, a_c_im, a_d, a_w_glu, kv_norm, w_k, w_v, w_f, b_f, b_norm, b_w_q,
              b_w_o, ffn_norm, ffn_w_up, ffn_conv_w, ffn_conv_b, ffn_w_down, final_norm):
    bsz, seq, _ = x.shape
    for i in range(N_A_LAYERS):
        x = x + s5_mixer(rms_norm(x, a_norm[i]), a_w_in[i], a_lambda_re[i], a_lambda_im[i],
                         a_log_step[i], a_b_re[i], a_b_im[i], a_c_re[i], a_c_im[i], a_d[i],
                         a_w_glu[i])
        x = x + conv_glu_ffn(rms_norm(x, ffn_norm[i]), ffn_w_up[i], ffn_conv_w[i],
                             ffn_conv_b[i], ffn_w_down[i])
    h_kv = rms_norm(x, kv_norm)
    k = (h_kv @ w_k).reshape(bsz, seq, FOX_HEADS, FOX_HEAD_DIM)
    v = (h_kv @ w_v).reshape(bsz, seq, FOX_HEADS, FOX_HEAD_DIM)
    log_f = jax.nn.log_sigmoid((h_kv @ w_f + b_f).astype(jnp.float32))
    logf_cum = jnp.cumsum(log_f, axis=1).transpose(0, 2, 1)
    for j in range(N_B_LAYERS):
        li = N_A_LAYERS + j
        x = x + fox_attention(rms_norm(x, b_norm[j]), b_w_q[j], b_w_o[j], k, v, logf_cum)
        x = x + conv_glu_ffn(rms_norm(x, ffn_norm[li]), ffn_w_up[li], ffn_conv_w[li],
                             ffn_conv_b[li], ffn_w_down[li])
    return rms_norm(x, final_norm)
```

```python
import functools
import math

import jax
import jax.numpy as jnp
from jax import lax
from jax.experimental import pallas as pl
from jax.experimental.pallas import tpu as pltpu

F32 = jnp.float32
BF16 = jnp.bfloat16

NORM_EPS = 1e-6
S5_GROUP = 16
S5_STATE = 64
FOX_HEAD_DIM = 128
CONV_WIDTH = 3

V7X_SUBLANES = 8
V7X_LANES = 128
V7X_VMEM_LIMIT_BYTES = 56 * 1024 * 1024

S5_GROUPS_PER_BLOCK = 16
MASK_VALUE = -1e30


def _cparams(semantics, vmem_bytes):
    budget = int(vmem_bytes) * 5 // 4 + (4 << 20)
    return pltpu.CompilerParams(
        dimension_semantics=semantics,
        vmem_limit_bytes=min(budget, V7X_VMEM_LIMIT_BYTES))


def _gelu_tanh(x):
    c = math.sqrt(2.0 / math.pi)
    return x * (0.5 * (1.0 + jnp.tanh(c * (x + 0.044715 * (x * x * x)))))


def _rms_normalize(x, g):
    ms = jnp.mean(x * x, axis=-1, keepdims=True)
    return x * lax.rsqrt(ms + NORM_EPS) * g


def _rmsnorm_kernel(x_ref, g_ref, o_ref):
    o_ref[...] = _rms_normalize(x_ref[...], g_ref[...]).astype(o_ref.dtype)


def _rmsnorm(x, g, out_dtype, tm=512):
    L, D = x.shape
    return pl.pallas_call(
        _rmsnorm_kernel,
        grid=(L // tm,),
        in_specs=[pl.BlockSpec((tm, D), lambda i: (i, 0)),
                  pl.BlockSpec((1, D), lambda i: (0, 0))],
        out_specs=pl.BlockSpec((tm, D), lambda i: (i, 0)),
        out_shape=jax.ShapeDtypeStruct((L, D), out_dtype),
        compiler_params=_cparams(("arbitrary",), 4 * tm * D * 8),
        name="rmsnorm",
    )(x, g.reshape(1, D))


def _matmul_kernel(a_ref, b_ref, o_ref):
    o_ref[...] = jnp.dot(a_ref[...], b_ref[...],
                         preferred_element_type=F32).astype(o_ref.dtype)


def _matmul_res_kernel(a_ref, b_ref, r_ref, o_ref):
    o_ref[...] = r_ref[...] + jnp.dot(a_ref[...], b_ref[...],
                                      preferred_element_type=F32)


def _matmul(a, b, out_dtype, residual=None, tm=1024, tn=512, name="matmul"):
    L, K = a.shape
    N = b.shape[1]
    in_specs = [pl.BlockSpec((tm, K), lambda i, j: (i, 0)),
                pl.BlockSpec((K, tn), lambda i, j: (0, j))]
    args = [a, b]
    kern = _matmul_kernel
    if residual is not None:
        in_specs.append(pl.BlockSpec((tm, tn), lambda i, j: (i, j)))
        args.append(residual)
        kern = _matmul_res_kernel
    vmem = 2 * (tm * K * 2 + K * tn * 2 + 2 * tm * tn * 4) + 2 * tm * tn * 4
    return pl.pallas_call(
        kern,
        grid=(L // tm, N // tn),
        in_specs=in_specs,
        out_specs=pl.BlockSpec((tm, tn), lambda i, j: (i, j)),
        out_shape=jax.ShapeDtypeStruct((L, N), out_dtype),
        compiler_params=_cparams(("arbitrary", "arbitrary"), vmem),
        name=name,
    )(*args)


def _s5_prepare(lam_re, lam_im, log_step, b_re, b_im, c_re, c_im, d):
    G, P = lam_re.shape
    C = b_re.shape[-1]
    gb = S5_GROUPS_PER_BLOCK
    nb = G // gb
    lam = lax.complex(lam_re.astype(F32), lam_im.astype(F32))
    dt = jnp.exp(log_step.astype(F32))[:, None]
    lam_bar = jnp.exp(lam * dt)
    b = lax.complex(b_re.astype(F32), b_im.astype(F32))
    b_bar = ((lam_bar - 1.0) / lam)[..., None] * b
    c = lax.complex(c_re.astype(F32), c_im.astype(F32))
    eye = jnp.eye(gb, dtype=F32)

    def in_proj(m):
        m = m.reshape(nb, gb, P, C)
        return jnp.einsum('ngpc,gh->ngchp', m, eye).reshape(nb, gb * C, gb * P)

    def out_proj(m):
        m = m.reshape(nb, gb, C, P)
        return jnp.einsum('ngcp,gh->ngphc', m, eye).reshape(nb, gb * P, gb * C)

    wb = jnp.concatenate([in_proj(jnp.real(b_bar)), in_proj(jnp.imag(b_bar))],
                         axis=-1).astype(BF16)
    wc = jnp.concatenate([out_proj(jnp.real(c)), out_proj(-jnp.imag(c))],
                         axis=-2).astype(BF16)

    pows = [lam_bar]
    for _ in range(V7X_SUBLANES - 1):
        pows.append(pows[-1] * lam_bar)
    pows = jnp.stack(pows, axis=0)
    rows = jnp.arange(V7X_SUBLANES)[:, None, None]

    def level(k):
        return jnp.where(rows >= k, pows[k - 1][None], 0.0)

    kinds = []
    for k in (1, 2, 4):
        a = level(k)
        kinds += [jnp.real(a), jnp.imag(a)]
    kinds += [jnp.real(pows), jnp.imag(pows)]
    cst = jnp.stack(kinds, axis=0).astype(F32)
    cst = cst.reshape(8, V7X_SUBLANES, nb, gb * P).transpose(2, 0, 1, 3)
    dvec = d.astype(F32).reshape(nb, 1, gb * C)
    return wb, wc, cst, dvec


def _s5_kernel(h_ref, win_ref, wb_ref, wc_ref, cst_ref, d_ref, o_ref,
               st_ref, carry_ref, *, tm, ns):
    @pl.when(pl.program_id(1) == 0)
    def _():
        carry_ref[...] = jnp.zeros_like(carry_ref)

    u = jnp.dot(h_ref[...], win_ref[...], preferred_element_type=F32)
    st_ref[...] = jnp.dot(u.astype(BF16), wb_ref[...],
                          preferred_element_type=F32)

    def tile(t, carry):
        cr, ci = carry
        r0 = pl.multiple_of(t * V7X_SUBLANES, V7X_SUBLANES)
        re = st_ref[pl.ds(r0, V7X_SUBLANES), 0:ns]
        im = st_ref[pl.ds(r0, V7X_SUBLANES), ns:2 * ns]
        for lvl, k in enumerate((1, 2, 4)):
            ar = cst_ref[2 * lvl]
            ai = cst_ref[2 * lvl + 1]
            sr = pltpu.roll(re, k, 0)
            si = pltpu.roll(im, k, 0)
            re, im = re + ar * sr - ai * si, im + ar * si + ai * sr
        pr = cst_ref[6]
        pi = cst_ref[7]
        re, im = re + pr * cr - pi * ci, im + pr * ci + pi * cr
        st_ref[pl.ds(r0, V7X_SUBLANES), 0:ns] = re
        st_ref[pl.ds(r0, V7X_SUBLANES), ns:2 * ns] = im
        return re[V7X_SUBLANES - 1:, :], im[V7X_SUBLANES - 1:, :]

    cr, ci = lax.fori_loop(0, tm // V7X_SUBLANES, tile,
                           (carry_ref[0:1, :], carry_ref[1:2, :]))
    carry_ref[0:1, :] = cr
    carry_ref[1:2, :] = ci

    y = jnp.dot(st_ref[...].astype(BF16), wc_ref[...],
                preferred_element_type=F32) + d_ref[...] * u
    o_ref[...] = _gelu_tanh(y).astype(o_ref.dtype)


def _s5_core(h, w_in, wb, wc, cst, dvec, tm=512):
    L, D = h.shape
    nb, cb, ns2 = wb.shape
    ns = ns2 // 2
    vmem = (2 * (tm * D * 2 + D * cb * 2 + 2 * cb * ns2 * 2 + 64 * ns * 4
                 + tm * cb * 2) + tm * ns2 * 4 * 2 + 4 * tm * cb * 4)
    return pl.pallas_call(
        functools.partial(_s5_kernel, tm=tm, ns=ns),
        grid=(nb, L // tm),
        in_specs=[
            pl.BlockSpec((tm, D), lambda j, i: (i, 0)),
            pl.BlockSpec((D, cb), lambda j, i: (0, j)),
            pl.BlockSpec((None, cb, ns2), lambda j, i: (j, 0, 0)),
            pl.BlockSpec((None, ns2, cb), lambda j, i: (j, 0, 0)),
            pl.BlockSpec((None, 8, V7X_SUBLANES, ns), lambda j, i: (j, 0, 0, 0)),
            pl.BlockSpec((None, 1, cb), lambda j, i: (j, 0, 0)),
        ],
        out_specs=pl.BlockSpec((tm, cb), lambda j, i: (i, j)),
        out_shape=jax.ShapeDtypeStruct((L, D), BF16),
        scratch_shapes=[pltpu.VMEM((tm, ns2), F32),
                        pltpu.VMEM((V7X_SUBLANES, ns), F32)],
        compiler_params=_cparams(("arbitrary", "arbitrary"), vmem),
        name="s5_core",
    )(h, w_in, wb, wc, cst, dvec)


def _glu_kernel(y_ref, wv_ref, wg_ref, x_ref, o_ref):
    y = y_ref[...]
    val = jnp.dot(y, wv_ref[...], preferred_element_type=F32)
    gate = jnp.dot(y, wg_ref[...], preferred_element_type=F32)
    o_ref[...] = x_ref[...] + val * (1.0 / (1.0 + jnp.exp(-gate)))


def _glu_residual(y, w_glu, x, tm=1024, tn=512):
    L, D = y.shape
    nj = D // tn
    vmem = 2 * (tm * D * 2 + 2 * D * tn * 2 + 2 * tm * tn * 4) + 4 * tm * tn * 4
    return pl.pallas_call(
        _glu_kernel,
        grid=(L // tm, nj),
        in_specs=[pl.BlockSpec((tm, D), lambda i, j: (i, 0)),
                  pl.BlockSpec((D, tn), lambda i, j: (0, j)),
                  pl.BlockSpec((D, tn), lambda i, j: (0, nj + j)),
                  pl.BlockSpec((tm, tn), lambda i, j: (i, j))],
        out_specs=pl.BlockSpec((tm, tn), lambda i, j: (i, j)),
        out_shape=jax.ShapeDtypeStruct((L, D), F32),
        compiler_params=_cparams(("arbitrary", "arbitrary"), vmem),
        name="s5_glu",
    )(y, w_glu, w_glu, x)


def _ffn_kernel(x_ref, g_ref, wg_ref, wv_ref, cw_ref, cb_ref, wd_ref, *rest,
                tm, final_norm):
    if final_norm:
        fg_ref, o_ref, h_ref, acc_ref, halo_ref, gs_ref = rest
    else:
        o_ref, h_ref, acc_ref, halo_ref, gs_ref = rest
    i = pl.program_id(0)
    f = pl.program_id(1)
    H = V7X_SUBLANES

    @pl.when(f == 0)
    def _():
        h_ref[...] = _rms_normalize(x_ref[...], g_ref[...]).astype(BF16)
        acc_ref[...] = jnp.zeros_like(acc_ref)

    h = h_ref[...]
    gate = jnp.dot(h, wg_ref[...], preferred_element_type=F32)
    val = jnp.dot(h, wv_ref[...], preferred_element_type=F32)

    prev = halo_ref[f]
    gs_ref[0:H, :] = jnp.where(i == 0, jnp.zeros_like(prev), prev)
    gs_ref[H:H + tm, :] = gate
    halo_ref[f] = gate[tm - H:, :]
    g1 = gs_ref[H - 1:H - 1 + tm, :]
    g2 = gs_ref[H - 2:H - 2 + tm, :]
    conv = (cw_ref[0:1, :] * g2 + cw_ref[1:2, :] * g1 + cw_ref[2:3, :] * gate
            + cb_ref[...])
    act = (_gelu_tanh(conv) * val).astype(BF16)
    acc_ref[...] += jnp.dot(act, wd_ref[...], preferred_element_type=F32)

    @pl.when(f == pl.num_programs(1) - 1)
    def _():
        y = x_ref[...] + acc_ref[...]
        if final_norm:
            y = _rms_normalize(y, fg_ref[...])
        o_ref[...] = y


def _ffn(x, g, w_up, conv_w, conv_b, w_down, final_g=None, tm=512, tf=512):
    L, D = x.shape
    F = w_down.shape[0]
    nf = F // tf
    final_norm = final_g is not None
    in_specs = [
        pl.BlockSpec((tm, D), lambda i, f: (i, 0)),
        pl.BlockSpec((1, D), lambda i, f: (0, 0)),
        pl.BlockSpec((D, tf), lambda i, f: (0, f)),
        pl.BlockSpec((D, tf), lambda i, f: (0, nf + f)),
        pl.BlockSpec((CONV_WIDTH, tf), lambda i, f: (0, f)),
        pl.BlockSpec((1, tf), lambda i, f: (0, f)),
        pl.BlockSpec((tf, D), lambda i, f: (f, 0)),
    ]
    args = [x, g.reshape(1, D), w_up, w_up, conv_w, conv_b.reshape(1, F), w_down]
    if final_norm:
        in_specs.append(pl.BlockSpec((1, D), lambda i, f: (0, 0)))
        args.append(final_g.reshape(1, D))
    vmem = (2 * (2 * tm * D * 4 + 3 * D * tf * 2) + tm * D * 2 + tm * D * 4
            + nf * V7X_SUBLANES * tf * 4 + (tm + 8) * tf * 4 + 6 * tm * tf * 4)
    return pl.pallas_call(
        functools.partial(_ffn_kernel, tm=tm, final_norm=final_norm),
        grid=(L // tm, nf),
        in_specs=in_specs,
        out_specs=pl.BlockSpec((tm, D), lambda i, f: (i, 0)),
        out_shape=jax.ShapeDtypeStruct((L, D), F32),
        scratch_shapes=[pltpu.VMEM((tm, D), BF16),
                        pltpu.VMEM((tm, D), F32),
                        pltpu.VMEM((nf, V7X_SUBLANES, tf), F32),
                        pltpu.VMEM((tm + V7X_SUBLANES, tf), F32)],
        compiler_params=_cparams(("arbitrary", "arbitrary"), vmem),
        name="conv_glu_ffn",
    )(*args)


def _fgate_kernel(wft_ref, h_ref, bf_ref, o_ref):
    z = lax.dot_general(wft_ref[...], h_ref[...], (((1,), (1,)), ((), ())),
                        preferred_element_type=F32)
    z = z + bf_ref[...]
    o_ref[...] = jnp.minimum(z, 0.0) - jnp.log1p(jnp.exp(-jnp.abs(z)))


def _fgate_logf(h_kv, w_f, b_f, tm=1024):
    L, D = h_kv.shape
    H = w_f.shape[1]
    hp = V7X_LANES
    wft = jnp.zeros((hp, D), BF16).at[:H].set(w_f.T.astype(BF16))
    bfp = jnp.zeros((hp, 1), F32).at[:H, 0].set(b_f.astype(F32))
    return pl.pallas_call(
        _fgate_kernel,
        grid=(L // tm,),
        in_specs=[pl.BlockSpec((hp, D), lambda i: (0, 0)),
                  pl.BlockSpec((tm, D), lambda i: (i, 0)),
                  pl.BlockSpec((hp, 1), lambda i: (0, 0))],
        out_specs=pl.BlockSpec((hp, tm), lambda i: (0, i)),
        out_shape=jax.ShapeDtypeStruct((hp, L), F32),
        compiler_params=_cparams(("arbitrary",), 2 * (tm * D * 2 + hp * D * 2)
                                 + 8 * hp * tm * 4),
        name="fgate_logf",
    )(wft, h_kv, bfp)


def _cumsum_kernel(x_ref, o_ref, *, tb):
    r = lax.broadcasted_iota(jnp.int32, (tb, tb), 0)
    c = lax.broadcasted_iota(jnp.int32, (tb, tb), 1)
    tri = (r <= c).astype(F32)

    def body(b, carry):
        off = pl.multiple_of(b * tb, tb)
        cs = jnp.dot(x_ref[:, pl.ds(off, tb)], tri,
                     precision=lax.Precision.HIGHEST,
                     preferred_element_type=F32) + carry
        o_ref[:, pl.ds(off, tb)] = cs
        return cs[:, tb - 1:tb]

    lax.fori_loop(0, x_ref.shape[1] // tb, body,
                  jnp.zeros((x_ref.shape[0], 1), F32))


def _cumsum_time(x, tb=256):
    R, L = x.shape
    return pl.pallas_call(
        functools.partial(_cumsum_kernel, tb=tb),
        out_shape=jax.ShapeDtypeStruct((R, L), F32),
        compiler_params=pltpu.CompilerParams(
            vmem_limit_bytes=min(6 * R * L * 4, V7X_VMEM_LIMIT_BYTES)),
        name="logf_cumsum",
    )(x)


def _attn_kernel(q_ref, k_ref, v_ref, ck_ref, o_ref, m_ref, l_ref, acc_ref,
                 *, bq, bk, scale):
    qi = pl.program_id(1)
    q = q_ref[...]
    m_ref[...] = jnp.full_like(m_ref, MASK_VALUE)
    l_ref[...] = jnp.zeros_like(l_ref)
    acc_ref[...] = jnp.zeros_like(acc_ref)

    def step(kj, masked):
        off = pl.multiple_of(kj * bk, bk)
        kb = k_ref[pl.ds(off, bk), :]
        vb = v_ref[pl.ds(off, bk), :]
        s = lax.dot_general(q, kb, (((1,), (1,)), ((), ())),
                            preferred_element_type=F32)
        t = s * scale - ck_ref[0, :, pl.ds(off, bk)]
        if masked:
            row = qi * bq + lax.broadcasted_iota(jnp.int32, (bq, bk), 0)
            col = off + lax.broadcasted_iota(jnp.int32, (bq, bk), 1)
            t = jnp.where(col <= row, t, MASK_VALUE)
        m_old = m_ref[...]
        m_new = jnp.maximum(m_old, jnp.max(t, axis=1, keepdims=True))
        alpha = jnp.exp(m_old - m_new)
        p = jnp.exp(t - m_new)
        l_ref[...] = alpha * l_ref[...] + jnp.sum(p, axis=1, keepdims=True)
        acc_ref[...] = alpha * acc_ref[...] + jnp.dot(
            p.astype(BF16), vb, preferred_element_type=F32)
        m_ref[...] = m_new

    n_full = qi * (bq // bk)

    def full_step(kj, c):
        step(kj, False)
        return c

    lax.fori_loop(0, n_full, full_step, 0)
    for dj in range(bq // bk):
        step(n_full + dj, True)
    o_ref[...] = (acc_ref[...] / l_ref[...]).astype(o_ref.dtype)


def _fox_attention(q, kv, ck, n_heads, bq=512, bk=512):
    L, D = q.shape
    dh = D // n_heads
    scale = dh ** -0.5
    vmem = 2 * (2 * L * dh * 2 + 2 * bq * dh * 2 + L * 4) + 10 * bq * bk * 4
    return pl.pallas_call(
        functools.partial(_attn_kernel, bq=bq, bk=bk, scale=scale),
        grid=(n_heads, L // bq),
        in_specs=[pl.BlockSpec((bq, dh), lambda h, i: (i, h)),
                  pl.BlockSpec((L, dh), lambda h, i: (0, h)),
                  pl.BlockSpec((L, dh), lambda h, i: (0, n_heads + h)),
                  pl.BlockSpec((1, 1, L), lambda h, i: (h, 0, 0))],
        out_specs=pl.BlockSpec((bq, dh), lambda h, i: (i, h)),
        out_shape=jax.ShapeDtypeStruct((L, D), BF16),
        scratch_shapes=[pltpu.VMEM((bq, 1), F32),
                        pltpu.VMEM((bq, 1), F32),
                        pltpu.VMEM((bq, dh), F32)],
        compiler_params=_cparams(("arbitrary", "arbitrary"), vmem),
        name="fox_attention",
    )(q, kv, kv, ck)


def kernel(x, a_norm, a_w_in, a_lambda_re, a_lambda_im, a_log_step, a_b_re, a_b_im, a_c_re, a_c_im, a_d, a_w_glu, kv_norm, w_k, w_v, w_f, b_f, b_norm, b_w_q, b_w_o, ffn_norm, ffn_w_up, ffn_conv_w, ffn_conv_b, ffn_w_down, final_norm):
    bsz, seq, d_model = x.shape
    assert bsz == 1
    n_a = a_w_in.shape[0]
    n_b = b_w_q.shape[0]
    n_heads = w_f.shape[1]
    xs = x.reshape(seq, d_model).astype(F32)

    def ffn(xs, li, final_g=None):
        return _ffn(xs, ffn_norm[li], ffn_w_up[li].astype(BF16), ffn_conv_w[li],
                    ffn_conv_b[li], ffn_w_down[li].astype(BF16), final_g=final_g)

    for i in range(n_a):
        h = _rmsnorm(xs, a_norm[i], BF16)
        wb, wc, cst, dvec = _s5_prepare(a_lambda_re[i], a_lambda_im[i], a_log_step[i],
                                        a_b_re[i], a_b_im[i], a_c_re[i], a_c_im[i],
                                        a_d[i])
        y = _s5_core(h, a_w_in[i].astype(BF16), wb, wc, cst, dvec)
        xs = _glu_residual(y, a_w_glu[i].astype(BF16), xs)
        xs = ffn(xs, i)

    h_kv = _rmsnorm(xs, kv_norm, BF16)
    w_kv = jnp.concatenate([w_k, w_v], axis=1).astype(BF16)
    kv = _matmul(h_kv, w_kv, BF16, name="kv_proj")
    logf_t = _fgate_logf(h_kv, w_f, b_f)
    ck = _cumsum_time(logf_t)[:n_heads].reshape(n_heads, 1, seq)

    for j in range(n_b):
        li = n_a + j
        h = _rmsnorm(xs, b_norm[j], BF16)
        q = _matmul(h, b_w_q[j].astype(BF16), BF16, name="q_proj")
        o = _fox_attention(q, kv, ck, n_heads)
        xs = _matmul(o, b_w_o[j].astype(BF16), F32, residual=xs, name="o_proj")
        xs = ffn(xs, li, final_g=final_norm if j == n_b - 1 else None)

    return xs.reshape(bsz, seq, d_model).astype(x.dtype)
```

```python
import functools
import math

import jax
import jax.numpy as jnp
from jax import lax
from jax.experimental import pallas as pl
from jax.experimental.pallas import tpu as pltpu

F32 = jnp.float32
BF16 = jnp.bfloat16

NORM_EPS = 1e-6
CONV_WIDTH = 3
LOG2E = math.log2(math.e)

V7X_SUBLANES = 8
V7X_LANES = 128
V7X_VMEM_LIMIT_BYTES = 56 * 1024 * 1024

S5_GROUPS_PER_BLOCK = 16
S5_ROW_BLOCK = 512
S5_SEG = S5_ROW_BLOCK // V7X_SUBLANES
FFN_ROW_CHUNK = 128
CK_PIECES = 3
MASK_VALUE = -1e30


def _cparams(semantics, vmem_bytes):
    budget = int(vmem_bytes) * 5 // 4 + (4 << 20)
    return pltpu.CompilerParams(
        dimension_semantics=semantics,
        vmem_limit_bytes=min(budget, V7X_VMEM_LIMIT_BYTES))


def _gelu_tanh(x):
    c = math.sqrt(2.0 / math.pi)
    return x * (0.5 * (1.0 + jnp.tanh(c * (x + 0.044715 * (x * x * x)))))


def _rms_normalize(x, g):
    ms = jnp.mean(x * x, axis=-1, keepdims=True)
    return x * lax.rsqrt(ms + NORM_EPS) * g


def _cmul(ar, ai, br, bi):
    return ar * br - ai * bi, ar * bi + ai * br


def _stacked(w):
    return w if w.ndim == 3 else w[None]


def _rmsnorm_kernel(x_ref, g_ref, *rest):
    o_ref = rest[-1]
    h = _rms_normalize(x_ref[...], g_ref[...]).astype(o_ref.dtype)
    if len(rest) == 2:
        h = jnp.dot(rest[0][...], h, preferred_element_type=F32).astype(o_ref.dtype)
    o_ref[...] = h


def _segment_interleave(tm, segment_rows):
    n_seg = tm // segment_rows
    r = jnp.arange(tm)
    src = (r % n_seg) * segment_rows + r // n_seg
    return (src[:, None] == jnp.arange(tm)[None, :]).astype(BF16)


def _rmsnorm(x, g, out_dtype, tm=512, segment_rows=None):
    L, D = x.shape
    in_specs = [pl.BlockSpec((tm, D), lambda i: (i, 0)),
                pl.BlockSpec((1, D), lambda i: (0, 0))]
    args = [x, g.reshape(1, D)]
    if segment_rows is not None:
        assert out_dtype == BF16
        in_specs.append(pl.BlockSpec((tm, tm), lambda i: (0, 0)))
        args.append(_segment_interleave(tm, segment_rows))
    return pl.pallas_call(
        _rmsnorm_kernel,
        grid=(L // tm,),
        in_specs=in_specs,
        out_specs=pl.BlockSpec((tm, D), lambda i: (i, 0)),
        out_shape=jax.ShapeDtypeStruct((L, D), out_dtype),
        compiler_params=_cparams(("arbitrary",), 4 * tm * D * 8),
        name="rmsnorm",
    )(*args)


def _matmul_kernel(a_ref, b_ref, o_ref, *, out_scale):
    acc = jnp.dot(a_ref[...], b_ref[...].astype(BF16), preferred_element_type=F32)
    if out_scale is not None:
        acc = acc * out_scale
    o_ref[...] = acc.astype(o_ref.dtype)


def _matmul_res_kernel(a_ref, b_ref, r_ref, o_ref):
    o_ref[...] = r_ref[...] + jnp.dot(a_ref[...], b_ref[...].astype(BF16),
                                      preferred_element_type=F32)


def _matmul(a, w, layer, out_dtype, residual=None, out_scale=None, tm=1024, tn=512,
            name="matmul"):
    L, K = a.shape
    w = _stacked(w)
    N = w.shape[2]
    in_specs = [pl.BlockSpec((tm, K), lambda i, j: (i, 0)),
                pl.BlockSpec((None, K, tn), lambda i, j: (layer, 0, j))]
    args = [a, w]
    if residual is not None:
        in_specs.append(pl.BlockSpec((tm, tn), lambda i, j: (i, j)))
        args.append(residual)
        kern = _matmul_res_kernel
    else:
        kern = functools.partial(_matmul_kernel, out_scale=out_scale)
    vmem = 2 * (tm * K * 2 + K * tn * 4 + 2 * tm * tn * 4) + K * tn * 2 + 2 * tm * tn * 4
    return pl.pallas_call(
        kern,
        grid=(L // tm, N // tn),
        in_specs=in_specs,
        out_specs=pl.BlockSpec((tm, tn), lambda i, j: (i, j)),
        out_shape=jax.ShapeDtypeStruct((L, N), out_dtype),
        compiler_params=_cparams(("arbitrary", "arbitrary"), vmem),
        name=name,
    )(*args)


def _matmul_nt_kernel(w_ref, a_ref, o_ref):
    o_ref[...] = lax.dot_general(
        w_ref[...].astype(BF16), a_ref[...], (((1,), (1,)), ((), ())),
        preferred_element_type=F32).astype(o_ref.dtype)


def _matmul_nt(w_t, a, out_dtype, tm=1024, tn=512, name="matmul_nt"):
    N, K = w_t.shape
    L = a.shape[0]
    vmem = 2 * (tm * K * 2 + K * tn * 4 + tm * tn * 4) + K * tn * 2 + 2 * tm * tn * 4
    return pl.pallas_call(
        _matmul_nt_kernel,
        grid=(L // tm, N // tn),
        in_specs=[pl.BlockSpec((tn, K), lambda i, j: (j, 0)),
                  pl.BlockSpec((tm, K), lambda i, j: (i, 0))],
        out_specs=pl.BlockSpec((tn, tm), lambda i, j: (j, i)),
        out_shape=jax.ShapeDtypeStruct((N, L), out_dtype),
        compiler_params=_cparams(("arbitrary", "arbitrary"), vmem),
        name=name,
    )(w_t, a)


def _cpowers(br, bi, n):
    pr, pi = br[None], bi[None]
    k = 1
    while k < n:
        tr, ti = _cmul(pr, pi, pr[k - 1:k], pi[k - 1:k])
        pr = jnp.concatenate([pr, tr], axis=0)
        pi = jnp.concatenate([pi, ti], axis=0)
        k *= 2
    return pr, pi


def _s5_prepare(lam_re, lam_im, log_step, b_re, b_im, c_re, c_im, d):
    G, P = lam_re.shape
    C = b_re.shape[-1]
    gb = S5_GROUPS_PER_BLOCK
    nb = G // gb
    S = V7X_SUBLANES
    lam_re = lam_re.astype(F32)
    lam_im = lam_im.astype(F32)
    dt = jnp.exp(log_step.astype(F32))[:, None]
    mag = jnp.exp(lam_re * dt)
    lbr = mag * jnp.cos(lam_im * dt)
    lbi = mag * jnp.sin(lam_im * dt)
    den = lam_re * lam_re + lam_im * lam_im
    fr = ((lbr - 1.0) * lam_re + lbi * lam_im) / den
    fi = (lbi * lam_re - (lbr - 1.0) * lam_im) / den
    bbr, bbi = _cmul(fr[..., None], fi[..., None], b_re.astype(F32), b_im.astype(F32))
    eye = jnp.eye(gb, dtype=F32)

    def in_proj(m):
        m = m.reshape(nb, gb, P, C)
        return jnp.einsum('ngpc,gh->ngchp', m, eye).reshape(nb, gb * C, gb * P)

    def out_proj(m):
        m = m.reshape(nb, gb, C, P)
        return jnp.einsum('ngcp,gh->ngphc', m, eye).reshape(nb, gb * P, gb * C)

    wb = jnp.concatenate([in_proj(bbr), in_proj(bbi)], axis=-1).astype(BF16)
    wc = jnp.concatenate([out_proj(c_re.astype(F32)), out_proj(-c_im.astype(F32))],
                         axis=-2).astype(BF16)

    def lanes(m):
        return m.reshape(m.shape[0], nb, gb * P).transpose(1, 0, 2)

    tr, ti = _cpowers(lbr, lbi, S5_SEG)
    tab = jnp.concatenate([lanes(tr), lanes(ti)], axis=-1)
    sr, si = _cpowers(tr[-1], ti[-1], S)
    rows = jnp.arange(S)[:, None, None]
    kinds = [jnp.broadcast_to(lbr[None], (S, G, P)),
             jnp.broadcast_to(lbi[None], (S, G, P))]
    for k in (1, 2, 4):
        kinds += [jnp.where(rows >= k, sr[k - 1][None], 0.0),
                  jnp.where(rows >= k, si[k - 1][None], 0.0)]
    kinds += [sr, si]
    cst = jnp.stack([lanes(m) for m in kinds], axis=1)
    dvec = d.astype(F32).reshape(nb, 1, gb * C)
    return wb, wc, cst, tab, dvec


def _s5_unpermutation(tm):
    l = jnp.arange(tm)
    src = (l % S5_SEG) * V7X_SUBLANES + l // S5_SEG
    return (src[:, None] == jnp.arange(tm)[None, :]).astype(BF16)


def _s5_kernel(h_ref, win_ref, wb_ref, wc_ref, cst_ref, tab_ref, d_ref,
               unperm_ref, o_ref, st_ref, carry_ref, *, tm, ns):
    S = V7X_SUBLANES

    @pl.when(pl.program_id(1) == 0)
    def _():
        carry_ref[...] = jnp.zeros_like(carry_ref)

    u = jnp.dot(h_ref[...], win_ref[...].astype(BF16),
                preferred_element_type=F32)
    st_ref[...] = jnp.dot(u.astype(BF16), wb_ref[...],
                          preferred_element_type=F32)

    ar = cst_ref[0]
    ai = cst_ref[1]

    def pass1(t, carry):
        xr, xi = carry
        r0 = pl.multiple_of(t * S, S)
        nr = ar * xr - ai * xi + st_ref[pl.ds(r0, S), 0:ns]
        ni = ar * xi + ai * xr + st_ref[pl.ds(r0, S), ns:2 * ns]
        st_ref[pl.ds(r0, S), 0:ns] = nr
        st_ref[pl.ds(r0, S), ns:2 * ns] = ni
        return nr, ni

    zero = jnp.zeros((S, ns), F32)
    dr, di = lax.fori_loop(0, tm // S, pass1, (zero, zero), unroll=2)

    for lvl, k in enumerate((1, 2, 4)):
        br = cst_ref[2 + 2 * lvl]
        bi = cst_ref[3 + 2 * lvl]
        sr = pltpu.roll(dr, k, 0)
        si = pltpu.roll(di, k, 0)
        dr, di = dr + br * sr - bi * si, di + br * si + bi * sr
    cin_r = carry_ref[0:1, :]
    cin_i = carry_ref[1:2, :]
    pr = cst_ref[8]
    pi = cst_ref[9]
    dr, di = dr + pr * cin_r - pi * cin_i, di + pr * cin_i + pi * cin_r
    first = lax.broadcasted_iota(jnp.int32, (S, ns), 0) == 0
    cr = jnp.where(first, cin_r, pltpu.roll(dr, 1, 0))
    ci = jnp.where(first, cin_i, pltpu.roll(di, 1, 0))
    carry_ref[0:1, :] = dr[S - 1:S, :]
    carry_ref[1:2, :] = di[S - 1:S, :]

    def pass2(t, c):
        r0 = pl.multiple_of(t * S, S)
        tr = tab_ref[pl.ds(t, 1), 0:ns]
        ti = tab_ref[pl.ds(t, 1), ns:2 * ns]
        st_ref[pl.ds(r0, S), 0:ns] += tr * cr - ti * ci
        st_ref[pl.ds(r0, S), ns:2 * ns] += tr * ci + ti * cr
        return c

    lax.fori_loop(0, tm // S, pass2, 0, unroll=2)

    y = jnp.dot(st_ref[...].astype(BF16), wc_ref[...],
                preferred_element_type=F32) + d_ref[...] * u
    act = _gelu_tanh(y).astype(BF16)
    o_ref[...] = jnp.dot(unperm_ref[...], act,
                         preferred_element_type=F32).astype(o_ref.dtype)


def _s5_core(h, w_in, layer, wb, wc, cst, tab, dvec):
    L, D = h.shape
    tm = S5_ROW_BLOCK
    nb, cb, ns2 = wb.shape
    ns = ns2 // 2
    vmem = (2 * (tm * D * 2 + D * cb * 4 + 2 * cb * ns2 * 2 + 10 * 8 * ns * 4
                 + S5_SEG * ns2 * 4 + tm * tm * 2 + tm * cb * 2)
            + tm * ns2 * 4 * 2 + 8 * tm * cb * 4)
    return pl.pallas_call(
        functools.partial(_s5_kernel, tm=tm, ns=ns),
        grid=(nb, L // tm),
        in_specs=[
            pl.BlockSpec((tm, D), lambda j, i: (i, 0)),
            pl.BlockSpec((None, D, cb), lambda j, i: (layer, 0, j)),
            pl.BlockSpec((None, cb, ns2), lambda j, i: (j, 0, 0)),
            pl.BlockSpec((None, ns2, cb), lambda j, i: (j, 0, 0)),
            pl.BlockSpec((None, 10, V7X_SUBLANES, ns), lambda j, i: (j, 0, 0, 0)),
            pl.BlockSpec((None, S5_SEG, ns2), lambda j, i: (j, 0, 0)),
            pl.BlockSpec((None, 1, cb), lambda j, i: (j, 0, 0)),
            pl.BlockSpec((tm, tm), lambda j, i: (0, 0)),
        ],
        out_specs=pl.BlockSpec((tm, cb), lambda j, i: (i, j)),
        out_shape=jax.ShapeDtypeStruct((L, D), BF16),
        scratch_shapes=[pltpu.VMEM((tm, ns2), F32),
                        pltpu.VMEM((V7X_SUBLANES, ns), F32)],
        compiler_params=_cparams(("arbitrary", "arbitrary"), vmem),
        name="s5_core",
    )(h, w_in, wb, wc, cst, tab, dvec, _s5_unpermutation(tm))


def _glu_kernel(y_ref, wv_ref, wg_ref, x_ref, o_ref):
    y = y_ref[...]
    val = jnp.dot(y, wv_ref[...].astype(BF16), preferred_element_type=F32)
    gate = jnp.dot(y, wg_ref[...].astype(BF16), preferred_element_type=F32)
    o_ref[...] = x_ref[...] + val * (1.0 / (1.0 + jnp.exp(-gate)))


def _glu_residual(y, w_glu, layer, x, tm=1024, tn=512):
    L, D = y.shape
    nj = D // tn
    vmem = (2 * (tm * D * 2 + 2 * D * tn * 4 + 2 * tm * tn * 4) + 2 * D * tn * 2
            + 4 * tm * tn * 4)
    return pl.pallas_call(
        _glu_kernel,
        grid=(L // tm, nj),
        in_specs=[pl.BlockSpec((tm, D), lambda i, j: (i, 0)),
                  pl.BlockSpec((None, D, tn), lambda i, j: (layer, 0, j)),
                  pl.BlockSpec((None, D, tn), lambda i, j: (layer, 0, nj + j)),
                  pl.BlockSpec((tm, tn), lambda i, j: (i, j))],
        out_specs=pl.BlockSpec((tm, tn), lambda i, j: (i, j)),
        out_shape=jax.ShapeDtypeStruct((L, D), F32),
        compiler_params=_cparams(("arbitrary", "arbitrary"), vmem),
        name="s5_glu",
    )(y, w_glu, w_glu, x)


def _ffn_kernel(x_ref, g_ref, wg_ref, wv_ref, cw_ref, cb_ref, wd_ref, *rest,
                tm, final_norm):
    if final_norm:
        fg_ref, o_ref, h_ref, halo_ref, gs_ref = rest
    else:
        o_ref, h_ref, halo_ref, gs_ref = rest
    i = pl.program_id(0)
    f = pl.program_id(1)
    H = V7X_SUBLANES

    def row_chunks(fn):
        def body(c, carry):
            fn(pl.ds(pl.multiple_of(c * FFN_ROW_CHUNK, FFN_ROW_CHUNK), FFN_ROW_CHUNK))
            return carry
        lax.fori_loop(0, tm // FFN_ROW_CHUNK, body, 0)

    @pl.when(f == 0)
    def _():
        def normalize(rows):
            h_ref[rows, :] = _rms_normalize(x_ref[rows, :], g_ref[...]).astype(BF16)
        row_chunks(normalize)

    h = h_ref[...]
    gate = jnp.dot(h, wg_ref[...].astype(BF16), preferred_element_type=F32)
    val = jnp.dot(h, wv_ref[...].astype(BF16), preferred_element_type=F32)

    prev = halo_ref[f]
    gs_ref[0:H, :] = jnp.where(i == 0, jnp.zeros_like(prev), prev)
    gs_ref[H:H + tm, :] = gate
    halo_ref[f] = gate[tm - H:, :]
    g1 = gs_ref[H - 1:H - 1 + tm, :]
    g2 = gs_ref[H - 2:H - 2 + tm, :]
    conv = (cw_ref[0:1, :] * g2 + cw_ref[1:2, :] * g1 + cw_ref[2:3, :] * gate
            + cb_ref[...])
    act = (_gelu_tanh(conv) * val).astype(BF16)
    part = jnp.dot(act, wd_ref[...].astype(BF16), preferred_element_type=F32)

    @pl.when(f == 0)
    def _():
        o_ref[...] = part

    @pl.when(f > 0)
    def _():
        o_ref[...] += part

    @pl.when(f == pl.num_programs(1) - 1)
    def _():
        def finish(rows):
            y = x_ref[rows, :] + o_ref[rows, :]
            if final_norm:
                y = _rms_normalize(y, fg_ref[...])
            o_ref[rows, :] = y
        row_chunks(finish)


def _ffn(x, layer, ffn_norm, w_up, conv_w, conv_b, w_down, final_g=None,
         tm=1024, tf=256):
    L, D = x.shape
    n_layers, F, _ = w_down.shape
    nf = F // tf
    final_norm = final_g is not None
    in_specs = [
        pl.BlockSpec((tm, D), lambda i, f: (i, 0), pipeline_mode=pl.Buffered(1)),
        pl.BlockSpec((None, 1, D), lambda i, f: (layer, 0, 0)),
        pl.BlockSpec((None, D, tf), lambda i, f: (layer, 0, f)),
        pl.BlockSpec((None, D, tf), lambda i, f: (layer, 0, nf + f)),
        pl.BlockSpec((None, CONV_WIDTH, tf), lambda i, f: (layer, 0, f)),
        pl.BlockSpec((None, 1, tf), lambda i, f: (layer, 0, f)),
        pl.BlockSpec((None, tf, D), lambda i, f: (layer, f, 0)),
    ]
    args = [x, ffn_norm.reshape(n_layers, 1, D), w_up, w_up, conv_w,
            conv_b.reshape(n_layers, 1, F), w_down]
    if final_norm:
        in_specs.append(pl.BlockSpec((1, D), lambda i, f: (0, 0)))
        args.append(final_g.reshape(1, D))
    vmem = (tm * D * 4 + 2 * tm * D * 4 + 2 * 3 * D * tf * 4 + 3 * D * tf * 2
            + tm * D * 2 + nf * V7X_SUBLANES * tf * 4 + (tm + 8) * tf * 4
            + 6 * tm * tf * 4)
    return pl.pallas_call(
        functools.partial(_ffn_kernel, tm=tm, final_norm=final_norm),
        grid=(L // tm, nf),
        in_specs=in_specs,
        out_specs=pl.BlockSpec((tm, D), lambda i, f: (i, 0)),
        out_shape=jax.ShapeDtypeStruct((L, D), F32),
        scratch_shapes=[pltpu.VMEM((tm, D), BF16),
                        pltpu.VMEM((nf, V7X_SUBLANES, tf), F32),
                        pltpu.VMEM((tm + V7X_SUBLANES, tf), F32)],
        compiler_params=_cparams(("arbitrary", "arbitrary"), vmem),
        name="conv_glu_ffn",
    )(*args)


def _fgate_kernel(wft_ref, h_ref, bf_ref, o_ref):
    z = lax.dot_general(wft_ref[...], h_ref[...], (((1,), (1,)), ((), ())),
                        preferred_element_type=F32)
    z = z + bf_ref[...]
    o_ref[...] = jnp.minimum(z, 0.0) - jnp.log1p(jnp.exp(-jnp.abs(z)))


def _fgate_logf(h_kv, w_f, b_f, tm=1024):
    L, D = h_kv.shape
    H = w_f.shape[1]
    hp = V7X_LANES
    wft = jnp.zeros((hp, D), BF16).at[:H].set(w_f.T.astype(BF16))
    bfp = jnp.zeros((hp, 1), F32).at[:H, 0].set(b_f.astype(F32))
    return pl.pallas_call(
        _fgate_kernel,
        grid=(L // tm,),
        in_specs=[pl.BlockSpec((hp, D), lambda i: (0, 0)),
                  pl.BlockSpec((tm, D), lambda i: (i, 0)),
                  pl.BlockSpec((hp, 1), lambda i: (0, 0))],
        out_specs=pl.BlockSpec((hp, tm), lambda i: (0, i)),
        out_shape=jax.ShapeDtypeStruct((hp, L), F32),
        compiler_params=_cparams(("arbitrary",), 2 * (tm * D * 2 + hp * D * 2)
                                 + 8 * hp * tm * 4),
        name="fgate_logf",
    )(wft, h_kv, bfp)


def _cumsum_kernel(x_ref, *o_refs, tb):
    r = lax.broadcasted_iota(jnp.int32, (tb, tb), 0)
    c = lax.broadcasted_iota(jnp.int32, (tb, tb), 1)
    tri = (r <= c).astype(F32)

    def body(b, carry):
        off = pl.multiple_of(b * tb, tb)
        cs = jnp.dot(x_ref[:, pl.ds(off, tb)], tri,
                     precision=lax.Precision.HIGHEST,
                     preferred_element_type=F32) + carry
        rem = cs * (-LOG2E)
        for o_ref in o_refs:
            piece = rem.astype(BF16)
            o_ref[:, pl.ds(off, tb)] = piece
            rem = rem - piece.astype(F32)
        return cs[:, tb - 1:tb]

    lax.fori_loop(0, x_ref.shape[1] // tb, body,
                  jnp.zeros((x_ref.shape[0], 1), F32))


def _cumsum_pieces(x, tb=256):
    R, L = x.shape
    return pl.pallas_call(
        functools.partial(_cumsum_kernel, tb=tb),
        out_shape=[jax.ShapeDtypeStruct((R, L), BF16)] * CK_PIECES,
        compiler_params=pltpu.CompilerParams(
            vmem_limit_bytes=min(8 * R * L * 4, V7X_VMEM_LIMIT_BYTES)),
        name="logf_cumsum",
    )(x)


def _attn_kernel(q_ref, k_ref, c_ref, vt_ref, o_ref, acc_ref, s_ref,
                 *, bq, bk, dh, sq):
    qi = pl.program_id(1)
    n_strips = bq // sq
    lane = lax.broadcasted_iota(jnp.int32, (sq, dh), 1)
    ones = (lane < CK_PIECES).astype(BF16)
    qa = [jnp.concatenate([q_ref[c * sq:(c + 1) * sq, :], ones], axis=1)
          for c in range(n_strips)]
    acc_ref[...] = jnp.zeros_like(acc_ref)

    def keys(kj):
        off = pl.multiple_of(kj * bk, bk)
        return jnp.concatenate([k_ref[pl.ds(off, bk), :], c_ref[pl.ds(off, bk), :]],
                               axis=1)

    def scores(ka, c):
        return lax.dot_general(ka, qa[c], (((1,), (1,)), ((), ())),
                               preferred_element_type=F32)

    assert n_strips % 2 == 0
    s_ref[0] = scores(keys(0), 0)

    def step(kj, carry, masked, last):
        off = pl.multiple_of(kj * bk, bk)
        ka = keys(kj)
        vt = vt_ref[:, pl.ds(off, bk)]
        out = []
        for c in range(n_strips):
            m, l = carry[c]
            if c + 1 < n_strips:
                s_ref[(c + 1) % 2] = scores(ka, c + 1)
            elif not last:
                s_ref[0] = scores(keys(kj + 1), 0)
            s = s_ref[c % 2]
            if masked:
                key = off + lax.broadcasted_iota(jnp.int32, (bk, sq), 0)
                qry = (qi * bq + c * sq
                       + lax.broadcasted_iota(jnp.int32, (bk, sq), 1))
                s = jnp.where(key <= qry, s, MASK_VALUE)
            m_new = jnp.maximum(m, jnp.max(s, axis=0, keepdims=True))
            alpha = jnp.exp2(m - m_new)
            p = jnp.exp2(s - m_new)
            l_new = alpha * l + jnp.sum(p, axis=0, keepdims=True)
            cols = slice(c * sq, (c + 1) * sq)
            acc_ref[:, cols] = alpha * acc_ref[:, cols] + jnp.dot(
                vt, p.astype(BF16), preferred_element_type=F32)
            out.append((m_new, l_new))
        return tuple(out)

    n_full = qi * (bq // bk)
    n_diag = bq // bk
    carry = tuple((jnp.full((1, sq), MASK_VALUE, F32), jnp.zeros((1, sq), F32))
                  for _ in range(n_strips))
    carry = lax.fori_loop(0, n_full, lambda kj, c: step(kj, c, False, False), carry)
    for dj in range(n_diag):
        carry = step(n_full + dj, carry, True, dj == n_diag - 1)
    l_all = jnp.concatenate([l for _, l in carry], axis=1)
    o_ref[...] = (acc_ref[...] / l_all).T.astype(o_ref.dtype)


def _fox_attention(q, k, caug, v_t, n_heads, bq=1024, bk=512, sq=256):
    L, D = q.shape
    dh = D // n_heads
    vmem = (2 * (3 * L * dh * 2 + 2 * bq * dh * 2) + dh * bq * 4 + 2 * bk * sq * 4
            + 8 * bk * sq * 4)
    return pl.pallas_call(
        functools.partial(_attn_kernel, bq=bq, bk=bk, dh=dh, sq=sq),
        grid=(n_heads, L // bq),
        in_specs=[pl.BlockSpec((bq, dh), lambda h, i: (i, h)),
                  pl.BlockSpec((L, dh), lambda h, i: (0, h)),
                  pl.BlockSpec((L, dh), lambda h, i: (0, h)),
                  pl.BlockSpec((dh, L), lambda h, i: (h, 0))],
        out_specs=pl.BlockSpec((bq, dh), lambda h, i: (i, h)),
        out_shape=jax.ShapeDtypeStruct((L, D), BF16),
        scratch_shapes=[pltpu.VMEM((dh, bq), F32), pltpu.VMEM((2, bk, sq), F32)],
        compiler_params=_cparams(("arbitrary", "arbitrary"), vmem),
        name="fox_attention",
    )(q, k, caug, v_t)


def kernel(x, a_norm, a_w_in, a_lambda_re, a_lambda_im, a_log_step, a_b_re, a_b_im, a_c_re, a_c_im, a_d, a_w_glu, kv_norm, w_k, w_v, w_f, b_f, b_norm, b_w_q, b_w_o, ffn_norm, ffn_w_up, ffn_conv_w, ffn_conv_b, ffn_w_down, final_norm):
    bsz, seq, d_model = x.shape
    n_a = a_w_in.shape[0]
    n_b = b_w_q.shape[0]
    n_heads = w_f.shape[1]
    dh = d_model // n_heads
    assert bsz == 1 and n_b >= 1
    xs = x.reshape(seq, d_model).astype(F32)

    def ffn(xs, li, final_g=None):
        return _ffn(xs, li, ffn_norm, ffn_w_up, ffn_conv_w, ffn_conv_b, ffn_w_down,
                    final_g=final_g)

    for i in range(n_a):
        h = _rmsnorm(xs, a_norm[i], BF16, tm=S5_ROW_BLOCK, segment_rows=S5_SEG)
        prep = _s5_prepare(a_lambda_re[i], a_lambda_im[i], a_log_step[i],
                           a_b_re[i], a_b_im[i], a_c_re[i], a_c_im[i], a_d[i])
        y = _s5_core(h, a_w_in, i, *prep)
        xs = _glu_residual(y, a_w_glu, i, xs)
        xs = ffn(xs, i)

    h_kv = _rmsnorm(xs, kv_norm, BF16)
    k = _matmul(h_kv, w_k, 0, BF16, name="k_proj")
    v_t = _matmul_nt(w_v.T, h_kv, BF16, name="vt_proj")
    pieces = _cumsum_pieces(_fgate_logf(h_kv, w_f, b_f))
    caug = jnp.stack([p[:n_heads] for p in pieces], axis=-1)
    caug = jnp.pad(caug.transpose(1, 0, 2), ((0, 0), (0, 0), (0, dh - CK_PIECES)))
    caug = caug.reshape(seq, d_model)

    q_scale = dh ** -0.5 * LOG2E
    for j in range(n_b):
        li = n_a + j
        h = _rmsnorm(xs, b_norm[j], BF16)
        q = _matmul(h, b_w_q, j, BF16, out_scale=q_scale, name="q_proj")
        o = _fox_attention(q, k, caug, v_t, n_heads)
        xs = _matmul(o, b_w_o, j, F32, residual=xs, name="o_proj")
        xs = ffn(xs, li, final_g=final_norm if j == n_b - 1 else None)

    return xs.reshape(bsz, seq, d_model).astype(x.dtype)
```

```python
import functools
import math

import jax
import jax.numpy as jnp
from jax import lax
from jax.experimental import pallas as pl
from jax.experimental.pallas import tpu as pltpu

F32 = jnp.float32
BF16 = jnp.bfloat16

NORM_EPS = 1e-6
CONV_WIDTH = 3
LOG2E = math.log2(math.e)

V7X_SUBLANES = 8
V7X_LANES = 128
V7X_VMEM_LIMIT_BYTES = 56 * 1024 * 1024

S5_GROUPS_PER_BLOCK = 16
S5_ROW_BLOCK = 512
S5_SEG = S5_ROW_BLOCK // V7X_SUBLANES
S5_CHAINS = 2
FFN_ROW_CHUNK = 128
FFN_COL_GROUP = 256
CK_PIECES = 3
MASK_VALUE = -1e30


def _cparams(semantics, vmem_bytes):
    budget = int(vmem_bytes) * 5 // 4 + (4 << 20)
    return pltpu.CompilerParams(
        dimension_semantics=semantics,
        vmem_limit_bytes=min(budget, V7X_VMEM_LIMIT_BYTES))


def _gelu_tanh(x):
    c = math.sqrt(2.0 / math.pi)
    return x * (0.5 * (1.0 + jnp.tanh(c * (x + 0.044715 * (x * x * x)))))


def _rms_normalize(x, g):
    ms = jnp.mean(x * x, axis=-1, keepdims=True)
    return x * lax.rsqrt(ms + NORM_EPS) * g


def _cmul(ar, ai, br, bi):
    return ar * br - ai * bi, ar * bi + ai * br


def _stacked(w):
    return w if w.ndim == 3 else w[None]


def _rmsnorm_kernel(x_ref, g_ref, *rest):
    o_ref = rest[-1]
    h = _rms_normalize(x_ref[...], g_ref[...]).astype(o_ref.dtype)
    if len(rest) == 2:
        h = jnp.dot(rest[0][...], h, preferred_element_type=F32).astype(o_ref.dtype)
    o_ref[...] = h


def _segment_interleave(tm, segment_rows):
    n_seg = tm // segment_rows
    r = jnp.arange(tm)
    src = (r % n_seg) * segment_rows + r // n_seg
    return (src[:, None] == jnp.arange(tm)[None, :]).astype(BF16)


def _rmsnorm(x, g, out_dtype, tm=512, segment_rows=None):
    L, D = x.shape
    in_specs = [pl.BlockSpec((tm, D), lambda i: (i, 0)),
                pl.BlockSpec((1, D), lambda i: (0, 0))]
    args = [x, g.reshape(1, D)]
    if segment_rows is not None:
        assert out_dtype == BF16
        in_specs.append(pl.BlockSpec((tm, tm), lambda i: (0, 0)))
        args.append(_segment_interleave(tm, segment_rows))
    return pl.pallas_call(
        _rmsnorm_kernel,
        grid=(L // tm,),
        in_specs=in_specs,
        out_specs=pl.BlockSpec((tm, D), lambda i: (i, 0)),
        out_shape=jax.ShapeDtypeStruct((L, D), out_dtype),
        compiler_params=_cparams(("arbitrary",), 4 * tm * D * 8),
        name="rmsnorm",
    )(*args)


def _matmul_kernel(a_ref, b_ref, o_ref, *, out_scale):
    acc = jnp.dot(a_ref[...], b_ref[...].astype(BF16), preferred_element_type=F32)
    if out_scale is not None:
        acc = acc * out_scale
    o_ref[...] = acc.astype(o_ref.dtype)


def _matmul_res_kernel(a_ref, b_ref, r_ref, o_ref):
    o_ref[...] = r_ref[...] + jnp.dot(a_ref[...], b_ref[...].astype(BF16),
                                      preferred_element_type=F32)


def _matmul(a, w, layer, out_dtype, residual=None, out_scale=None, tm=2048, tn=512,
            name="matmul"):
    L, K = a.shape
    w = _stacked(w)
    N = w.shape[2]
    in_specs = [pl.BlockSpec((tm, K), lambda i, j: (i, 0)),
                pl.BlockSpec((None, K, tn), lambda i, j: (layer, 0, j))]
    args = [a, w]
    if residual is not None:
        in_specs.append(pl.BlockSpec((tm, tn), lambda i, j: (i, j)))
        args.append(residual)
        kern = _matmul_res_kernel
    else:
        kern = functools.partial(_matmul_kernel, out_scale=out_scale)
    vmem = 2 * (tm * K * 2 + K * tn * 4 + 2 * tm * tn * 4) + K * tn * 2 + 2 * tm * tn * 4
    return pl.pallas_call(
        kern,
        grid=(L // tm, N // tn),
        in_specs=in_specs,
        out_specs=pl.BlockSpec((tm, tn), lambda i, j: (i, j)),
        out_shape=jax.ShapeDtypeStruct((L, N), out_dtype),
        compiler_params=_cparams(("arbitrary", "arbitrary"), vmem),
        name=name,
    )(*args)


def _cpowers(br, bi, n):
    pr, pi = br[None], bi[None]
    k = 1
    while k < n:
        tr, ti = _cmul(pr, pi, pr[k - 1:k], pi[k - 1:k])
        pr = jnp.concatenate([pr, tr], axis=0)
        pi = jnp.concatenate([pi, ti], axis=0)
        k *= 2
    return pr, pi


def _s5_prepare(lam_re, lam_im, log_step, b_re, b_im, c_re, c_im, d):
    G, P = lam_re.shape
    C = b_re.shape[-1]
    gb = S5_GROUPS_PER_BLOCK
    nb = G // gb
    S = V7X_SUBLANES
    lam_re = lam_re.astype(F32)
    lam_im = lam_im.astype(F32)
    dt = jnp.exp(log_step.astype(F32))[:, None]
    mag = jnp.exp(lam_re * dt)
    lbr = mag * jnp.cos(lam_im * dt)
    lbi = mag * jnp.sin(lam_im * dt)
    den = lam_re * lam_re + lam_im * lam_im
    fr = ((lbr - 1.0) * lam_re + lbi * lam_im) / den
    fi = (lbi * lam_re - (lbr - 1.0) * lam_im) / den
    bbr, bbi = _cmul(fr[..., None], fi[..., None], b_re.astype(F32), b_im.astype(F32))
    eye = jnp.eye(gb, dtype=F32)

    def in_proj(m):
        m = m.reshape(nb, gb, P, C)
        return jnp.einsum('ngpc,gh->ngchp', m, eye).reshape(nb, gb * C, gb * P)

    def out_proj(m):
        m = m.reshape(nb, gb, C, P)
        return jnp.einsum('ngcp,gh->ngphc', m, eye).reshape(nb, gb * P, gb * C)

    wb = jnp.concatenate([in_proj(bbr), in_proj(bbi)], axis=-1).astype(BF16)
    wc = jnp.concatenate([out_proj(c_re.astype(F32)), out_proj(-c_im.astype(F32))],
                         axis=-2).astype(BF16)

    def lanes(m):
        return m.reshape(m.shape[0], nb, gb * P).transpose(1, 0, 2)

    tr, ti = _cpowers(lbr, lbi, S5_SEG)
    tab = jnp.concatenate([lanes(tr), lanes(ti)], axis=-1)
    sr, si = _cpowers(tr[-1], ti[-1], S)
    rows = jnp.arange(S)[:, None, None]
    kinds = [jnp.broadcast_to(lbr[None], (S, G, P)),
             jnp.broadcast_to(lbi[None], (S, G, P))]
    for k in (1, 2, 4):
        kinds += [jnp.where(rows >= k, sr[k - 1][None], 0.0),
                  jnp.where(rows >= k, si[k - 1][None], 0.0)]
    kinds += [sr, si]
    cst = jnp.stack([lanes(m) for m in kinds], axis=1)
    dvec = d.astype(F32).reshape(nb, 1, gb * C)
    return wb, wc, cst, tab, dvec


def _s5_unpermutation(tm):
    l = jnp.arange(tm)
    src = (l % S5_SEG) * V7X_SUBLANES + l // S5_SEG
    return (src[:, None] == jnp.arange(tm)[None, :]).astype(BF16)


def _s5_kernel(h_ref, win_ref, wb_ref, wc_ref, cst_ref, tab_ref, d_ref,
               unperm_ref, o_ref, st_ref, carry_ref, *, tm, ns, cb):
    S = V7X_SUBLANES
    n_tiles = tm // S

    @pl.when(pl.program_id(1) == 0)
    def _():
        carry_ref[...] = jnp.zeros_like(carry_ref)

    h = h_ref[...]

    def project(c):
        u = jnp.dot(h, win_ref[:, c * cb:(c + 1) * cb].astype(BF16),
                    preferred_element_type=F32)
        st_ref[c] = jnp.dot(u.astype(BF16), wb_ref[c],
                            preferred_element_type=F32)
        return u

    def scan(c):
        ar = cst_ref[c, 0]
        ai = cst_ref[c, 1]
        dr = jnp.zeros((S, ns), F32)
        di = jnp.zeros((S, ns), F32)
        for t in range(n_tiles):
            rows = slice(t * S, (t + 1) * S)
            dr, di = (ar * dr - ai * di + st_ref[c, rows, 0:ns],
                      ar * di + ai * dr + st_ref[c, rows, ns:2 * ns])
            st_ref[c, rows, 0:ns] = dr
            st_ref[c, rows, ns:2 * ns] = di
        for lvl, k in enumerate((1, 2, 4)):
            br = cst_ref[c, 2 + 2 * lvl]
            bi = cst_ref[c, 3 + 2 * lvl]
            sr = pltpu.roll(dr, k, 0)
            si = pltpu.roll(di, k, 0)
            dr, di = dr + br * sr - bi * si, di + br * si + bi * sr
        cin_r = carry_ref[c, 0:1, :]
        cin_i = carry_ref[c, 1:2, :]
        pr = cst_ref[c, 8]
        pi = cst_ref[c, 9]
        dr, di = dr + pr * cin_r - pi * cin_i, di + pr * cin_i + pi * cin_r
        first = lax.broadcasted_iota(jnp.int32, (S, ns), 0) == 0
        cr = jnp.where(first, cin_r, pltpu.roll(dr, 1, 0))
        ci = jnp.where(first, cin_i, pltpu.roll(di, 1, 0))
        carry_ref[c, 0:1, :] = dr[S - 1:S, :]
        carry_ref[c, 1:2, :] = di[S - 1:S, :]
        for t in range(n_tiles):
            rows = slice(t * S, (t + 1) * S)
            tr = tab_ref[c, t:t + 1, 0:ns]
            ti = tab_ref[c, t:t + 1, ns:2 * ns]
            st_ref[c, rows, 0:ns] += tr * cr - ti * ci
            st_ref[c, rows, ns:2 * ns] += tr * ci + ti * cr

    def emit(c, u):
        y = jnp.dot(st_ref[c].astype(BF16), wc_ref[c],
                    preferred_element_type=F32) + d_ref[c] * u
        act = _gelu_tanh(y).astype(BF16)
        o_ref[:, c * cb:(c + 1) * cb] = jnp.dot(
            unperm_ref[...], act, preferred_element_type=F32).astype(o_ref.dtype)

    us = [project(c) for c in range(S5_CHAINS)]
    for c in range(S5_CHAINS):
        scan(c)
        emit(c, us[c])


def _s5_core(h, w_in, layer, wb, wc, cst, tab, dvec):
    L, D = h.shape
    tm = S5_ROW_BLOCK
    nc = S5_CHAINS
    nb, cb, ns2 = wb.shape
    ns = ns2 // 2
    vmem = (2 * (tm * D * 2 + nc * (D * cb * 4 + 2 * cb * ns2 * 2 + 10 * 8 * ns * 4
                                    + S5_SEG * ns2 * 4 + tm * cb * 2) + tm * tm * 2)
            + nc * (tm * ns2 * 4 + D * cb * 2 + 4 * tm * cb * 4) + tm * ns2 * 4)
    return pl.pallas_call(
        functools.partial(_s5_kernel, tm=tm, ns=ns, cb=cb),
        grid=(nb // nc, L // tm),
        in_specs=[
            pl.BlockSpec((tm, D), lambda j, i: (i, 0)),
            pl.BlockSpec((None, D, nc * cb), lambda j, i: (layer, 0, j)),
            pl.BlockSpec((nc, cb, ns2), lambda j, i: (j, 0, 0)),
            pl.BlockSpec((nc, ns2, cb), lambda j, i: (j, 0, 0)),
            pl.BlockSpec((nc, 10, V7X_SUBLANES, ns), lambda j, i: (j, 0, 0, 0)),
            pl.BlockSpec((nc, S5_SEG, ns2), lambda j, i: (j, 0, 0)),
            pl.BlockSpec((nc, 1, cb), lambda j, i: (j, 0, 0)),
            pl.BlockSpec((tm, tm), lambda j, i: (0, 0)),
        ],
        out_specs=pl.BlockSpec((tm, nc * cb), lambda j, i: (i, j)),
        out_shape=jax.ShapeDtypeStruct((L, D), BF16),
        scratch_shapes=[pltpu.VMEM((nc, tm, ns2), F32),
                        pltpu.VMEM((nc, V7X_SUBLANES, ns), F32)],
        compiler_params=_cparams(("arbitrary", "arbitrary"), vmem),
        name="s5_core",
    )(h, w_in, wb, wc, cst, tab, dvec, _s5_unpermutation(tm))


def _glu_kernel(y_ref, wv_ref, wg_ref, x_ref, o_ref):
    y = y_ref[...]
    val = jnp.dot(y, wv_ref[...].astype(BF16), preferred_element_type=F32)
    gate = jnp.dot(y, wg_ref[...].astype(BF16), preferred_element_type=F32)
    o_ref[...] = x_ref[...] + val * (1.0 / (1.0 + jnp.exp(-gate)))


def _glu_residual(y, w_glu, layer, x, tm=2048, tn=256):
    L, D = y.shape
    nj = D // tn
    vmem = (2 * (tm * D * 2 + 2 * D * tn * 4 + 2 * tm * tn * 4) + 2 * D * tn * 2
            + 4 * tm * tn * 4)
    return pl.pallas_call(
        _glu_kernel,
        grid=(L // tm, nj),
        in_specs=[pl.BlockSpec((tm, D), lambda i, j: (i, 0)),
                  pl.BlockSpec((None, D, tn), lambda i, j: (layer, 0, j)),
                  pl.BlockSpec((None, D, tn), lambda i, j: (layer, 0, nj + j)),
                  pl.BlockSpec((tm, tn), lambda i, j: (i, j))],
        out_specs=pl.BlockSpec((tm, tn), lambda i, j: (i, j)),
        out_shape=jax.ShapeDtypeStruct((L, D), F32),
        compiler_params=_cparams(("arbitrary", "arbitrary"), vmem),
        name="s5_glu",
    )(y, w_glu, w_glu, x)


def _ffn_kernel(x_ref, g_ref, wg_ref, wv_ref, cw_ref, cb_ref, wd_ref, *rest,
                tm, tf, final_norm):
    if final_norm:
        fg_ref, o_ref, h_ref, halo_ref, gs_ref = rest
    else:
        o_ref, h_ref, halo_ref, gs_ref = rest
    i = pl.program_id(0)
    f = pl.program_id(1)
    H = V7X_SUBLANES

    def row_chunks(fn):
        def body(c, carry):
            fn(pl.ds(pl.multiple_of(c * FFN_ROW_CHUNK, FFN_ROW_CHUNK), FFN_ROW_CHUNK))
            return carry
        lax.fori_loop(0, tm // FFN_ROW_CHUNK, body, 0)

    @pl.when(f == 0)
    def _():
        def normalize(rows):
            h_ref[rows, :] = _rms_normalize(x_ref[rows, :], g_ref[...]).astype(BF16)
            o_ref[rows, :] = jnp.zeros((FFN_ROW_CHUNK, o_ref.shape[1]), F32)
        row_chunks(normalize)

    h = h_ref[...]

    def up(cols):
        gate = jnp.dot(h, wg_ref[:, cols].astype(BF16), preferred_element_type=F32)
        val = jnp.dot(h, wv_ref[:, cols].astype(BF16), preferred_element_type=F32)
        return gate, val

    def activate(cols, gate, val):
        prev = halo_ref[f, :, cols]
        gs_ref[0:H, cols] = jnp.where(i == 0, jnp.zeros_like(prev), prev)
        gs_ref[H:H + tm, cols] = gate
        halo_ref[f, :, cols] = gate[tm - H:, :]
        g1 = gs_ref[H - 1:H - 1 + tm, cols]
        g2 = gs_ref[H - 2:H - 2 + tm, cols]
        conv = (cw_ref[0:1, cols] * g2 + cw_ref[1:2, cols] * g1
                + cw_ref[2:3, cols] * gate + cb_ref[:, cols])
        return (_gelu_tanh(conv) * val).astype(BF16)

    groups = [slice(c, c + FFN_COL_GROUP) for c in range(0, tf, FFN_COL_GROUP)]
    ups = [up(cols) for cols in groups]
    part = None
    for cols, (gate, val) in zip(groups, ups):
        act = activate(cols, gate, val)
        d = jnp.dot(act, wd_ref[cols, :].astype(BF16), preferred_element_type=F32)
        part = d if part is None else part + d
    o_ref[...] += part

    @pl.when(f == pl.num_programs(1) - 1)
    def _():
        def finish(rows):
            y = x_ref[rows, :] + o_ref[rows, :]
            if final_norm:
                y = _rms_normalize(y, fg_ref[...])
            o_ref[rows, :] = y
        row_chunks(finish)


def _ffn(x, layer, ffn_norm, w_up, conv_w, conv_b, w_down, final_g=None,
         tm=512, tf=512):
    L, D = x.shape
    n_layers, F, _ = w_down.shape
    nf = F // tf
    final_norm = final_g is not None
    in_specs = [
        pl.BlockSpec((tm, D), lambda i, f: (i, 0), pipeline_mode=pl.Buffered(1)),
        pl.BlockSpec((None, 1, D), lambda i, f: (layer, 0, 0)),
        pl.BlockSpec((None, D, tf), lambda i, f: (layer, 0, f)),
        pl.BlockSpec((None, D, tf), lambda i, f: (layer, 0, nf + f)),
        pl.BlockSpec((None, CONV_WIDTH, tf), lambda i, f: (layer, 0, f)),
        pl.BlockSpec((None, 1, tf), lambda i, f: (layer, 0, f)),
        pl.BlockSpec((None, tf, D), lambda i, f: (layer, f, 0)),
    ]
    args = [x, ffn_norm.reshape(n_layers, 1, D), w_up, w_up, conv_w,
            conv_b.reshape(n_layers, 1, F), w_down]
    if final_norm:
        in_specs.append(pl.BlockSpec((1, D), lambda i, f: (0, 0)))
        args.append(final_g.reshape(1, D))
    vmem = (tm * D * 4 + 2 * tm * D * 4 + 2 * 3 * D * tf * 4 + 3 * D * tf * 2
            + tm * D * 2 + nf * V7X_SUBLANES * tf * 4 + (tm + 8) * tf * 4
            + 6 * tm * tf * 4)
    return pl.pallas_call(
        functools.partial(_ffn_kernel, tm=tm, tf=tf, final_norm=final_norm),
        grid=(L // tm, nf),
        in_specs=in_specs,
        out_specs=pl.BlockSpec((tm, D), lambda i, f: (i, 0)),
        out_shape=jax.ShapeDtypeStruct((L, D), F32),
        scratch_shapes=[pltpu.VMEM((tm, D), BF16),
                        pltpu.VMEM((nf, V7X_SUBLANES, tf), F32),
                        pltpu.VMEM((tm + V7X_SUBLANES, tf), F32)],
        compiler_params=_cparams(("arbitrary", "arbitrary"), vmem),
        name="conv_glu_ffn",
    )(*args)


def _fgate_kernel(wft_ref, h_ref, bf_ref, o_ref):
    z = lax.dot_general(wft_ref[...], h_ref[...], (((1,), (1,)), ((), ())),
                        preferred_element_type=F32)
    z = z + bf_ref[...]
    o_ref[...] = jnp.minimum(z, 0.0) - jnp.log1p(jnp.exp(-jnp.abs(z)))


def _fgate_logf(h_kv, w_f, b_f, tm=1024):
    L, D = h_kv.shape
    H = w_f.shape[1]
    hp = V7X_LANES
    wft = jnp.zeros((hp, D), BF16).at[:H].set(w_f.T.astype(BF16))
    bfp = jnp.zeros((hp, 1), F32).at[:H, 0].set(b_f.astype(F32))
    return pl.pallas_call(
        _fgate_kernel,
        grid=(L // tm,),
        in_specs=[pl.BlockSpec((hp, D), lambda i: (0, 0)),
                  pl.BlockSpec((tm, D), lambda i: (i, 0)),
                  pl.BlockSpec((hp, 1), lambda i: (0, 0))],
        out_specs=pl.BlockSpec((hp, tm), lambda i: (0, i)),
        out_shape=jax.ShapeDtypeStruct((hp, L), F32),
        compiler_params=_cparams(("arbitrary",), 2 * (tm * D * 2 + hp * D * 2)
                                 + 8 * hp * tm * 4),
        name="fgate_logf",
    )(wft, h_kv, bfp)


def _cumsum_kernel(x_ref, *o_refs, tb):
    r = lax.broadcasted_iota(jnp.int32, (tb, tb), 0)
    c = lax.broadcasted_iota(jnp.int32, (tb, tb), 1)
    tri = (r <= c).astype(F32)

    def body(b, carry):
        off = pl.multiple_of(b * tb, tb)
        cs = jnp.dot(x_ref[:, pl.ds(off, tb)], tri,
                     precision=lax.Precision.HIGHEST,
                     preferred_element_type=F32) + carry
        rem = cs * (-LOG2E)
        for o_ref in o_refs:
            piece = rem.astype(BF16)
            o_ref[:, pl.ds(off, tb)] = piece
            rem = rem - piece.astype(F32)
        return cs[:, tb - 1:tb]

    lax.fori_loop(0, x_ref.shape[1] // tb, body,
                  jnp.zeros((x_ref.shape[0], 1), F32))


def _cumsum_pieces(x, tb=256):
    R, L = x.shape
    return pl.pallas_call(
        functools.partial(_cumsum_kernel, tb=tb),
        out_shape=[jax.ShapeDtypeStruct((R, L), BF16)] * CK_PIECES,
        compiler_params=pltpu.CompilerParams(
            vmem_limit_bytes=min(8 * R * L * 4, V7X_VMEM_LIMIT_BYTES)),
        name="logf_cumsum",
    )(x)


def _attn_kernel(q_ref, k_ref, c_ref, v_ref, o_ref, acc_ref, m_ref, l_ref, s_ref,
                 *, bq, bk, dh):
    qi = pl.program_id(1)
    W = V7X_LANES
    n_lane_tiles = bk // W
    lane = lax.broadcasted_iota(jnp.int32, (bq, dh), 1)
    qa = jnp.concatenate([q_ref[...], (lane < CK_PIECES).astype(BF16)], axis=1)
    acc_ref[...] = jnp.zeros_like(acc_ref)
    m_ref[...] = jnp.full_like(m_ref, MASK_VALUE)
    l_ref[...] = jnp.zeros_like(l_ref)

    def scores(kj):
        off = pl.multiple_of(kj * bk, bk)
        ka = jnp.concatenate([k_ref[pl.ds(off, bk), :], c_ref[pl.ds(off, bk), :]],
                             axis=1)
        return lax.dot_general(qa, ka, (((1,), (1,)), ((), ())),
                               preferred_element_type=F32)

    s_ref[0] = scores(0)

    def step(kj, slot, masked, last):
        off = pl.multiple_of(kj * bk, bk)
        if not last:
            s_ref[1 - slot] = scores(kj + 1)
        tiles = []
        for t in range(n_lane_tiles):
            s = s_ref[slot, :, t * W:(t + 1) * W]
            if masked:
                key = off + t * W + lax.broadcasted_iota(jnp.int32, (bq, W), 1)
                qry = qi * bq + lax.broadcasted_iota(jnp.int32, (bq, W), 0)
                s = jnp.where(key <= qry, s, MASK_VALUE)
            tiles.append(s)
        tile_max = functools.reduce(jnp.maximum, tiles)
        row_max = jnp.broadcast_to(jnp.max(tile_max, axis=1, keepdims=True), (bq, W))
        m_old = m_ref[...]
        m_new = jnp.maximum(m_old, row_max)
        alpha = jnp.exp2(m_old - m_new)
        m_ref[...] = m_new
        p = [jnp.exp2(s - m_new) for s in tiles]
        l_ref[...] = alpha * l_ref[...] + functools.reduce(lambda a, b: a + b, p)
        pv = jnp.dot(jnp.concatenate([x.astype(BF16) for x in p], axis=1),
                     v_ref[pl.ds(off, bk), :], preferred_element_type=F32)
        acc_ref[...] = alpha * acc_ref[...] + pv

    n_diag = bq // bk
    assert n_diag % 2 == 0 and dh == W

    def pair(jj, carry):
        step(2 * jj, 0, False, False)
        step(2 * jj + 1, 1, False, False)
        return carry

    lax.fori_loop(0, qi * (n_diag // 2), pair, 0)
    for dj in range(n_diag):
        step(qi * n_diag + dj, dj % 2, True, dj == n_diag - 1)
    l = jnp.sum(l_ref[...], axis=1, keepdims=True)
    o_ref[...] = (acc_ref[...] / l).astype(o_ref.dtype)


def _fox_attention(q, k, v, caug, n_heads, bq=1024, bk=512):
    L, D = q.shape
    dh = D // n_heads
    vmem = (2 * (3 * L * dh * 2 + 2 * bq * dh * 2) + 3 * bq * dh * 4 + 2 * bq * bk * 4
            + 4 * bq * bk * 4)
    return pl.pallas_call(
        functools.partial(_attn_kernel, bq=bq, bk=bk, dh=dh),
        grid=(n_heads, L // bq),
        in_specs=[pl.BlockSpec((bq, dh), lambda h, i: (i, h)),
                  pl.BlockSpec((L, dh), lambda h, i: (0, h)),
                  pl.BlockSpec((L, dh), lambda h, i: (0, h)),
                  pl.BlockSpec((L, dh), lambda h, i: (0, h))],
        out_specs=pl.BlockSpec((bq, dh), lambda h, i: (i, h)),
        out_shape=jax.ShapeDtypeStruct((L, D), BF16),
        scratch_shapes=[pltpu.VMEM((bq, dh), F32), pltpu.VMEM((bq, V7X_LANES), F32),
                        pltpu.VMEM((bq, V7X_LANES), F32),
                        pltpu.VMEM((2, bq, bk), F32)],
        compiler_params=_cparams(("arbitrary", "arbitrary"), vmem),
        name="fox_attention",
    )(q, k, caug, v)


def kernel(x, a_norm, a_w_in, a_lambda_re, a_lambda_im, a_log_step, a_b_re, a_b_im, a_c_re, a_c_im, a_d, a_w_glu, kv_norm, w_k, w_v, w_f, b_f, b_norm, b_w_q, b_w_o, ffn_norm, ffn_w_up, ffn_conv_w, ffn_conv_b, ffn_w_down, final_norm):
    bsz, seq, d_model = x.shape
    n_a = a_w_in.shape[0]
    n_b = b_w_q.shape[0]
    n_heads = w_f.shape[1]
    dh = d_model // n_heads
    assert bsz == 1 and n_b >= 1
    xs = x.reshape(seq, d_model).astype(F32)

    def ffn(xs, li, final_g=None):
        return _ffn(xs, li, ffn_norm, ffn_w_up, ffn_conv_w, ffn_conv_b, ffn_w_down,
                    final_g=final_g)

    for i in range(n_a):
        h = _rmsnorm(xs, a_norm[i], BF16, tm=S5_ROW_BLOCK, segment_rows=S5_SEG)
        prep = _s5_prepare(a_lambda_re[i], a_lambda_im[i], a_log_step[i],
                           a_b_re[i], a_b_im[i], a_c_re[i], a_c_im[i], a_d[i])
        y = _s5_core(h, a_w_in, i, *prep)
        xs = _glu_residual(y, a_w_glu, i, xs)
        xs = ffn(xs, i)

    h_kv = _rmsnorm(xs, kv_norm, BF16)
    k = _matmul(h_kv, w_k, 0, BF16, name="k_proj")
    v = _matmul(h_kv, w_v, 0, BF16, name="v_proj")
    pieces = _cumsum_pieces(_fgate_logf(h_kv, w_f, b_f))
    caug = jnp.stack([p[:n_heads] for p in pieces], axis=-1)
    caug = jnp.pad(caug.transpose(1, 0, 2), ((0, 0), (0, 0), (0, dh - CK_PIECES)))
    caug = caug.reshape(seq, d_model)

    q_scale = dh ** -0.5 * LOG2E
    for j in range(n_b):
        li = n_a + j
        h = _rmsnorm(xs, b_norm[j], BF16)
        q = _matmul(h, b_w_q, j, BF16, out_scale=q_scale, name="q_proj")
        o = _fox_attention(q, k, v, caug, n_heads)
        xs = _matmul(o, b_w_o, j, F32, residual=xs, name="o_proj")
        xs = ffn(xs, li, final_g=final_norm if j == n_b - 1 else None)

    return xs.reshape(bsz, seq, d_model).astype(x.dtype)
```

```python
import functools
import math

import jax
import jax.numpy as jnp
from jax import lax
from jax.experimental import pallas as pl
from jax.experimental.pallas import tpu as pltpu

F32 = jnp.float32
BF16 = jnp.bfloat16

NORM_EPS = 1e-6
CONV_WIDTH = 3
LOG2E = math.log2(math.e)

V7X_SUBLANES = 8
V7X_LANES = 128
V7X_VMEM_LIMIT_BYTES = 56 * 1024 * 1024

S5_GROUPS_PER_BLOCK = 16
S5_ROW_BLOCK = 512
S5_SEG = S5_ROW_BLOCK // V7X_SUBLANES
S5_CHAINS = 2
FFN_ROW_CHUNK = 128
FFN_COL_GROUP = 256
CK_PIECES = 3
MASK_VALUE = -1e30


def _cparams(semantics, vmem_bytes):
    budget = int(vmem_bytes) * 5 // 4 + (4 << 20)
    return pltpu.CompilerParams(
        dimension_semantics=semantics,
        vmem_limit_bytes=min(budget, V7X_VMEM_LIMIT_BYTES))


def _gelu_tanh(x):
    c = math.sqrt(2.0 / math.pi)
    return x * (0.5 * (1.0 + jnp.tanh(c * (x + 0.044715 * (x * x * x)))))


def _rms_normalize(x, g):
    ms = jnp.mean(x * x, axis=-1, keepdims=True)
    return x * lax.rsqrt(ms + NORM_EPS) * g


def _cmul(ar, ai, br, bi):
    return ar * br - ai * bi, ar * bi + ai * br


def _stacked(w):
    return w if w.ndim == 3 else w[None]


def _cast_kernel(w_ref, o_ref):
    o_ref[...] = w_ref[...].astype(o_ref.dtype)


def _cast_bf16(w, tr=512, max_cols=4096):
    shape = w.shape
    C = shape[-1]
    R = math.prod(shape[:-1])
    tc = C
    while tc > max_cols:
        tc //= 2
    assert C % tc == 0 and tc % V7X_LANES == 0 and R % tr == 0
    out = pl.pallas_call(
        _cast_kernel,
        grid=(R // tr, C // tc),
        in_specs=[pl.BlockSpec((tr, tc), lambda i, j: (i, j))],
        out_specs=pl.BlockSpec((tr, tc), lambda i, j: (i, j)),
        out_shape=jax.ShapeDtypeStruct((R, C), BF16),
        compiler_params=_cparams(("arbitrary", "arbitrary"), 2 * tr * tc * 6),
        name="cast_bf16",
    )(w.reshape(R, C))
    return out.reshape(shape)


def _rmsnorm_kernel(x_ref, g_ref, *rest):
    o_ref = rest[-1]
    h = _rms_normalize(x_ref[...], g_ref[...]).astype(o_ref.dtype)
    if len(rest) == 2:
        h = jnp.dot(rest[0][...], h, preferred_element_type=F32).astype(o_ref.dtype)
    o_ref[...] = h


def _segment_interleave(tm, segment_rows):
    n_seg = tm // segment_rows
    r = jnp.arange(tm)
    src = (r % n_seg) * segment_rows + r // n_seg
    return (src[:, None] == jnp.arange(tm)[None, :]).astype(BF16)


def _rmsnorm(x, g, out_dtype, tm=512, segment_rows=None):
    L, D = x.shape
    in_specs = [pl.BlockSpec((tm, D), lambda i: (i, 0)),
                pl.BlockSpec((1, D), lambda i: (0, 0))]
    args = [x, g.reshape(1, D)]
    if segment_rows is not None:
        assert out_dtype == BF16
        in_specs.append(pl.BlockSpec((tm, tm), lambda i: (0, 0)))
        args.append(_segment_interleave(tm, segment_rows))
    return pl.pallas_call(
        _rmsnorm_kernel,
        grid=(L // tm,),
        in_specs=in_specs,
        out_specs=pl.BlockSpec((tm, D), lambda i: (i, 0)),
        out_shape=jax.ShapeDtypeStruct((L, D), out_dtype),
        compiler_params=_cparams(("arbitrary",), 4 * tm * D * 8),
        name="rmsnorm",
    )(*args)


def _matmul_kernel(a_ref, b_ref, o_ref, *, out_scale):
    acc = jnp.dot(a_ref[...], b_ref[...].astype(BF16), preferred_element_type=F32)
    if out_scale is not None:
        acc = acc * out_scale
    o_ref[...] = acc.astype(o_ref.dtype)


def _matmul_res_kernel(a_ref, b_ref, r_ref, o_ref):
    o_ref[...] = r_ref[...] + jnp.dot(a_ref[...], b_ref[...].astype(BF16),
                                      preferred_element_type=F32)


def _matmul(a, w, layer, out_dtype, residual=None, out_scale=None, tm=2048, tn=512,
            name="matmul"):
    L, K = a.shape
    w = _stacked(w)
    N = w.shape[2]
    in_specs = [pl.BlockSpec((tm, K), lambda i, j: (i, 0)),
                pl.BlockSpec((None, K, tn), lambda i, j: (layer, 0, j))]
    args = [a, w]
    if residual is not None:
        in_specs.append(pl.BlockSpec((tm, tn), lambda i, j: (i, j)))
        args.append(residual)
        kern = _matmul_res_kernel
    else:
        kern = functools.partial(_matmul_kernel, out_scale=out_scale)
    vmem = 2 * (tm * K * 2 + K * tn * 4 + 2 * tm * tn * 4) + K * tn * 2 + 2 * tm * tn * 4
    return pl.pallas_call(
        kern,
        grid=(L // tm, N // tn),
        in_specs=in_specs,
        out_specs=pl.BlockSpec((tm, tn), lambda i, j: (i, j)),
        out_shape=jax.ShapeDtypeStruct((L, N), out_dtype),
        compiler_params=_cparams(("arbitrary", "arbitrary"), vmem),
        name=name,
    )(*args)


def _cpowers(br, bi, n):
    pr, pi = br[None], bi[None]
    k = 1
    while k < n:
        tr, ti = _cmul(pr, pi, pr[k - 1:k], pi[k - 1:k])
        pr = jnp.concatenate([pr, tr], axis=0)
        pi = jnp.concatenate([pi, ti], axis=0)
        k *= 2
    return pr, pi


def _s5_prepare(lam_re, lam_im, log_step, b_re, b_im, c_re, c_im, d):
    G, P = lam_re.shape
    C = b_re.shape[-1]
    gb = S5_GROUPS_PER_BLOCK
    nb = G // gb
    S = V7X_SUBLANES
    lam_re = lam_re.astype(F32)
    lam_im = lam_im.astype(F32)
    dt = jnp.exp(log_step.astype(F32))[:, None]
    mag = jnp.exp(lam_re * dt)
    lbr = mag * jnp.cos(lam_im * dt)
    lbi = mag * jnp.sin(lam_im * dt)
    den = lam_re * lam_re + lam_im * lam_im
    fr = ((lbr - 1.0) * lam_re + lbi * lam_im) / den
    fi = (lbi * lam_re - (lbr - 1.0) * lam_im) / den
    bbr, bbi = _cmul(fr[..., None], fi[..., None], b_re.astype(F32), b_im.astype(F32))
    eye = jnp.eye(gb, dtype=F32)

    def in_proj(m):
        m = m.reshape(nb, gb, P, C)
        return jnp.einsum('ngpc,gh->ngchp', m, eye).reshape(nb, gb * C, gb * P)

    def out_proj(m):
        m = m.reshape(nb, gb, C, P)
        return jnp.einsum('ngcp,gh->ngphc', m, eye).reshape(nb, gb * P, gb * C)

    wb = jnp.concatenate([in_proj(bbr), in_proj(bbi)], axis=-1).astype(BF16)
    wc = jnp.concatenate([out_proj(c_re.astype(F32)), out_proj(-c_im.astype(F32))],
                         axis=-2).astype(BF16)

    def lanes(m):
        return m.reshape(m.shape[0], nb, gb * P).transpose(1, 0, 2)

    tr, ti = _cpowers(lbr, lbi, S5_SEG)
    tab = jnp.concatenate([lanes(tr), lanes(ti)], axis=-1)
    sr, si = _cpowers(tr[-1], ti[-1], S)
    rows = jnp.arange(S)[:, None, None]
    kinds = [jnp.broadcast_to(lbr[None], (S, G, P)),
             jnp.broadcast_to(lbi[None], (S, G, P))]
    for k in (1, 2, 4):
        kinds += [jnp.where(rows >= k, sr[k - 1][None], 0.0),
                  jnp.where(rows >= k, si[k - 1][None], 0.0)]
    kinds += [sr, si]
    cst = jnp.stack([lanes(m) for m in kinds], axis=1)
    dvec = d.astype(F32).reshape(nb, 1, gb * C)
    return wb, wc, cst, tab, dvec


def _s5_unpermutation(tm):
    l = jnp.arange(tm)
    src = (l % S5_SEG) * V7X_SUBLANES + l // S5_SEG
    return (src[:, None] == jnp.arange(tm)[None, :]).astype(BF16)


def _s5_kernel(h_ref, win_ref, wb_ref, wc_ref, cst_ref, tab_ref, d_ref,
               unperm_ref, o_ref, st_ref, carry_ref, *, tm, ns, cb):
    S = V7X_SUBLANES
    n_tiles = tm // S

    @pl.when(pl.program_id(1) == 0)
    def _():
        carry_ref[...] = jnp.zeros_like(carry_ref)

    h = h_ref[...]

    def project(c):
        u = jnp.dot(h, win_ref[:, c * cb:(c + 1) * cb].astype(BF16),
                    preferred_element_type=F32)
        st_ref[c] = jnp.dot(u.astype(BF16), wb_ref[c],
                            preferred_element_type=F32)
        return u

    def scan(c):
        ar = cst_ref[c, 0]
        ai = cst_ref[c, 1]
        dr = jnp.zeros((S, ns), F32)
        di = jnp.zeros((S, ns), F32)
        for t in range(n_tiles):
            rows = slice(t * S, (t + 1) * S)
            dr, di = (ar * dr - ai * di + st_ref[c, rows, 0:ns],
                      ar * di + ai * dr + st_ref[c, rows, ns:2 * ns])
            st_ref[c, rows, 0:ns] = dr
            st_ref[c, rows, ns:2 * ns] = di
        for lvl, k in enumerate((1, 2, 4)):
            br = cst_ref[c, 2 + 2 * lvl]
            bi = cst_ref[c, 3 + 2 * lvl]
            sr = pltpu.roll(dr, k, 0)
            si = pltpu.roll(di, k, 0)
            dr, di = dr + br * sr - bi * si, di + br * si + bi * sr
        cin_r = carry_ref[c, 0:1, :]
        cin_i = carry_ref[c, 1:2, :]
        pr = cst_ref[c, 8]
        pi = cst_ref[c, 9]
        dr, di = dr + pr * cin_r - pi * cin_i, di + pr * cin_i + pi * cin_r
        first = lax.broadcasted_iota(jnp.int32, (S, ns), 0) == 0
        cr = jnp.where(first, cin_r, pltpu.roll(dr, 1, 0))
        ci = jnp.where(first, cin_i, pltpu.roll(di, 1, 0))
        carry_ref[c, 0:1, :] = dr[S - 1:S, :]
        carry_ref[c, 1:2, :] = di[S - 1:S, :]
        for t in range(n_tiles):
            rows = slice(t * S, (t + 1) * S)
            tr = tab_ref[c, t:t + 1, 0:ns]
            ti = tab_ref[c, t:t + 1, ns:2 * ns]
            st_ref[c, rows, 0:ns] += tr * cr - ti * ci
            st_ref[c, rows, ns:2 * ns] += tr * ci + ti * cr

    def emit(c, u):
        y = jnp.dot(st_ref[c].astype(BF16), wc_ref[c],
                    preferred_element_type=F32) + d_ref[c] * u
        act = _gelu_tanh(y).astype(BF16)
        o_ref[:, c * cb:(c + 1) * cb] = jnp.dot(
            unperm_ref[...], act, preferred_element_type=F32).astype(o_ref.dtype)

    us = [project(c) for c in range(S5_CHAINS)]
    for c in range(S5_CHAINS):
        scan(c)
        emit(c, us[c])


def _s5_core(h, w_in, layer, wb, wc, cst, tab, dvec):
    L, D = h.shape
    tm = S5_ROW_BLOCK
    nc = S5_CHAINS
    nb, cb, ns2 = wb.shape
    ns = ns2 // 2
    vmem = (2 * (tm * D * 2 + nc * (D * cb * 4 + 2 * cb * ns2 * 2 + 10 * 8 * ns * 4
                                    + S5_SEG * ns2 * 4 + tm * cb * 2) + tm * tm * 2)
            + nc * (tm * ns2 * 4 + D * cb * 2 + 4 * tm * cb * 4) + tm * ns2 * 4)
    return pl.pallas_call(
        functools.partial(_s5_kernel, tm=tm, ns=ns, cb=cb),
        grid=(nb // nc, L // tm),
        in_specs=[
            pl.BlockSpec((tm, D), lambda j, i: (i, 0)),
            pl.BlockSpec((None, D, nc * cb), lambda j, i: (layer, 0, j)),
            pl.BlockSpec((nc, cb, ns2), lambda j, i: (j, 0, 0)),
            pl.BlockSpec((nc, ns2, cb), lambda j, i: (j, 0, 0)),
            pl.BlockSpec((nc, 10, V7X_SUBLANES, ns), lambda j, i: (j, 0, 0, 0)),
            pl.BlockSpec((nc, S5_SEG, ns2), lambda j, i: (j, 0, 0)),
            pl.BlockSpec((nc, 1, cb), lambda j, i: (j, 0, 0)),
            pl.BlockSpec((tm, tm), lambda j, i: (0, 0)),
        ],
        out_specs=pl.BlockSpec((tm, nc * cb), lambda j, i: (i, j)),
        out_shape=jax.ShapeDtypeStruct((L, D), BF16),
        scratch_shapes=[pltpu.VMEM((nc, tm, ns2), F32),
                        pltpu.VMEM((nc, V7X_SUBLANES, ns), F32)],
        compiler_params=_cparams(("arbitrary", "arbitrary"), vmem),
        name="s5_core",
    )(h, w_in, wb, wc, cst, tab, dvec, _s5_unpermutation(tm))


def _glu_kernel(y_ref, wv_ref, wg_ref, x_ref, o_ref):
    y = y_ref[...]
    val = jnp.dot(y, wv_ref[...], preferred_element_type=F32)
    gate = jnp.dot(y, wg_ref[...], preferred_element_type=F32)
    o_ref[...] = x_ref[...] + val * (1.0 / (1.0 + jnp.exp(-gate)))


def _glu_residual(y, w_glu, layer, x, tm=2048, tn=256):
    L, D = y.shape
    nj = D // tn
    vmem = 2 * (tm * D * 2 + 2 * D * tn * 2 + 2 * tm * tn * 4) + 4 * tm * tn * 4
    return pl.pallas_call(
        _glu_kernel,
        grid=(L // tm, nj),
        in_specs=[pl.BlockSpec((tm, D), lambda i, j: (i, 0)),
                  pl.BlockSpec((None, D, tn), lambda i, j: (layer, 0, j)),
                  pl.BlockSpec((None, D, tn), lambda i, j: (layer, 0, nj + j)),
                  pl.BlockSpec((tm, tn), lambda i, j: (i, j))],
        out_specs=pl.BlockSpec((tm, tn), lambda i, j: (i, j)),
        out_shape=jax.ShapeDtypeStruct((L, D), F32),
        compiler_params=_cparams(("arbitrary", "arbitrary"), vmem),
        name="s5_glu",
    )(y, w_glu, w_glu, x)


def _ffn_kernel(x_ref, g_ref, wg_ref, wv_ref, cw_ref, cb_ref, wd_ref, *rest,
                tm, tf, final_norm):
    if final_norm:
        fg_ref, o_ref, h_ref, halo_ref, gs_ref = rest
    else:
        o_ref, h_ref, halo_ref, gs_ref = rest
    i = pl.program_id(0)
    f = pl.program_id(1)
    H = V7X_SUBLANES

    def row_chunks(fn):
        def body(c, carry):
            fn(pl.ds(pl.multiple_of(c * FFN_ROW_CHUNK, FFN_ROW_CHUNK), FFN_ROW_CHUNK))
            return carry
        lax.fori_loop(0, tm // FFN_ROW_CHUNK, body, 0)

    @pl.when(f == 0)
    def _():
        def normalize(rows):
            h_ref[rows, :] = _rms_normalize(x_ref[rows, :], g_ref[...]).astype(BF16)
            o_ref[rows, :] = jnp.zeros((FFN_ROW_CHUNK, o_ref.shape[1]), F32)
        row_chunks(normalize)

    h = h_ref[...]

    def up(cols):
        gate = jnp.dot(h, wg_ref[:, cols], preferred_element_type=F32)
        val = jnp.dot(h, wv_ref[:, cols], preferred_element_type=F32)
        return gate, val

    def activate(cols, gate, val):
        prev = halo_ref[f, :, cols]
        gs_ref[0:H, cols] = jnp.where(i == 0, jnp.zeros_like(prev), prev)
        gs_ref[H:H + tm, cols] = gate
        halo_ref[f, :, cols] = gate[tm - H:, :]
        g1 = gs_ref[H - 1:H - 1 + tm, cols]
        g2 = gs_ref[H - 2:H - 2 + tm, cols]
        conv = (cw_ref[0:1, cols] * g2 + cw_ref[1:2, cols] * g1
                + cw_ref[2:3, cols] * gate + cb_ref[:, cols])
        return (_gelu_tanh(conv) * val).astype(BF16)

    groups = [slice(c, c + FFN_COL_GROUP) for c in range(0, tf, FFN_COL_GROUP)]
    ups = [up(cols) for cols in groups]
    part = None
    for cols, (gate, val) in zip(groups, ups):
        act = activate(cols, gate, val)
        d = jnp.dot(act, wd_ref[cols, :], preferred_element_type=F32)
        part = d if part is None else part + d
    o_ref[...] += part

    @pl.when(f == pl.num_programs(1) - 1)
    def _():
        def finish(rows):
            y = x_ref[rows, :] + o_ref[rows, :]
            if final_norm:
                y = _rms_normalize(y, fg_ref[...])
            o_ref[rows, :] = y
        row_chunks(finish)


def _ffn(x, layer, ffn_norm, w_up, conv_w, conv_b, w_down, final_g=None,
         tm=512, tf=512):
    L, D = x.shape
    n_layers, F, _ = w_down.shape
    nf = F // tf
    final_norm = final_g is not None
    in_specs = [
        pl.BlockSpec((tm, D), lambda i, f: (i, 0), pipeline_mode=pl.Buffered(1)),
        pl.BlockSpec((None, 1, D), lambda i, f: (layer, 0, 0)),
        pl.BlockSpec((None, D, tf), lambda i, f: (layer, 0, f)),
        pl.BlockSpec((None, D, tf), lambda i, f: (layer, 0, nf + f)),
        pl.BlockSpec((None, CONV_WIDTH, tf), lambda i, f: (layer, 0, f)),
        pl.BlockSpec((None, 1, tf), lambda i, f: (layer, 0, f)),
        pl.BlockSpec((None, tf, D), lambda i, f: (layer, f, 0)),
    ]
    args = [x, ffn_norm.reshape(n_layers, 1, D), w_up, w_up, conv_w,
            conv_b.reshape(n_layers, 1, F), w_down]
    if final_norm:
        in_specs.append(pl.BlockSpec((1, D), lambda i, f: (0, 0)))
        args.append(final_g.reshape(1, D))
    vmem = (tm * D * 4 + 2 * tm * D * 4 + 2 * 3 * D * tf * 2 + tm * D * 2
            + nf * V7X_SUBLANES * tf * 4 + (tm + 8) * tf * 4 + 6 * tm * tf * 4
            + tm * D * 4)
    return pl.pallas_call(
        functools.partial(_ffn_kernel, tm=tm, tf=tf, final_norm=final_norm),
        grid=(L // tm, nf),
        in_specs=in_specs,
        out_specs=pl.BlockSpec((tm, D), lambda i, f: (i, 0)),
        out_shape=jax.ShapeDtypeStruct((L, D), F32),
        scratch_shapes=[pltpu.VMEM((tm, D), BF16),
                        pltpu.VMEM((nf, V7X_SUBLANES, tf), F32),
                        pltpu.VMEM((tm + V7X_SUBLANES, tf), F32)],
        compiler_params=_cparams(("arbitrary", "arbitrary"), vmem),
        name="conv_glu_ffn",
    )(*args)


def _fgate_kernel(wft_ref, h_ref, bf_ref, o_ref):
    z = lax.dot_general(wft_ref[...], h_ref[...], (((1,), (1,)), ((), ())),
                        preferred_element_type=F32)
    z = z + bf_ref[...]
    o_ref[...] = jnp.minimum(z, 0.0) - jnp.log1p(jnp.exp(-jnp.abs(z)))


def _fgate_logf(h_kv, w_f, b_f, tm=1024):
    L, D = h_kv.shape
    H = w_f.shape[1]
    hp = V7X_LANES
    wft = jnp.zeros((hp, D), BF16).at[:H].set(w_f.T.astype(BF16))
    bfp = jnp.zeros((hp, 1), F32).at[:H, 0].set(b_f.astype(F32))
    return pl.pallas_call(
        _fgate_kernel,
        grid=(L // tm,),
        in_specs=[pl.BlockSpec((hp, D), lambda i: (0, 0)),
                  pl.BlockSpec((tm, D), lambda i: (i, 0)),
                  pl.BlockSpec((hp, 1), lambda i: (0, 0))],
        out_specs=pl.BlockSpec((hp, tm), lambda i: (0, i)),
        out_shape=jax.ShapeDtypeStruct((hp, L), F32),
        compiler_params=_cparams(("arbitrary",), 2 * (tm * D * 2 + hp * D * 2)
                                 + 8 * hp * tm * 4),
        name="fgate_logf",
    )(wft, h_kv, bfp)


def _cumsum_kernel(x_ref, *o_refs, tb):
    r = lax.broadcasted_iota(jnp.int32, (tb, tb), 0)
    c = lax.broadcasted_iota(jnp.int32, (tb, tb), 1)
    tri = (r <= c).astype(F32)

    def body(b, carry):
        off = pl.multiple_of(b * tb, tb)
        cs = jnp.dot(x_ref[:, pl.ds(off, tb)], tri,
                     precision=lax.Precision.HIGHEST,
                     preferred_element_type=F32) + carry
        rem = cs * (-LOG2E)
        for o_ref in o_refs:
            piece = rem.astype(BF16)
            o_ref[:, pl.ds(off, tb)] = piece
            rem = rem - piece.astype(F32)
        return cs[:, tb - 1:tb]

    lax.fori_loop(0, x_ref.shape[1] // tb, body,
                  jnp.zeros((x_ref.shape[0], 1), F32))


def _cumsum_pieces(x, tb=256):
    R, L = x.shape
    return pl.pallas_call(
        functools.partial(_cumsum_kernel, tb=tb),
        out_shape=[jax.ShapeDtypeStruct((R, L), BF16)] * CK_PIECES,
        compiler_params=pltpu.CompilerParams(
            vmem_limit_bytes=min(8 * R * L * 4, V7X_VMEM_LIMIT_BYTES)),
        name="logf_cumsum",
    )(x)


def _attn_kernel(q_ref, k_ref, c_ref, v_ref, o_ref, acc_ref, m_ref, l_ref, s_ref,
                 *, bq, bk, dh):
    qi = pl.program_id(1)
    W = V7X_LANES
    n_lane_tiles = bk // W
    lane = lax.broadcasted_iota(jnp.int32, (bq, dh), 1)
    qa = jnp.concatenate([q_ref[...], (lane < CK_PIECES).astype(BF16)], axis=1)
    acc_ref[...] = jnp.zeros_like(acc_ref)
    m_ref[...] = jnp.full_like(m_ref, MASK_VALUE)
    l_ref[...] = jnp.zeros_like(l_ref)

    def scores(kj):
        off = pl.multiple_of(kj * bk, bk)
        ka = jnp.concatenate([k_ref[pl.ds(off, bk), :], c_ref[pl.ds(off, bk), :]],
                             axis=1)
        return lax.dot_general(qa, ka, (((1,), (1,)), ((), ())),
                               preferred_element_type=F32)

    s_ref[0] = scores(0)

    def step(kj, slot, masked, last):
        off = pl.multiple_of(kj * bk, bk)
        if not last:
            s_ref[1 - slot] = scores(kj + 1)
        tiles = []
        for t in range(n_lane_tiles):
            s = s_ref[slot, :, t * W:(t + 1) * W]
            if masked:
                key = off + t * W + lax.broadcasted_iota(jnp.int32, (bq, W), 1)
                qry = qi * bq + lax.broadcasted_iota(jnp.int32, (bq, W), 0)
                s = jnp.where(key <= qry, s, MASK_VALUE)
            tiles.append(s)
        tile_max = functools.reduce(jnp.maximum, tiles)
        row_max = jnp.broadcast_to(jnp.max(tile_max, axis=1, keepdims=True), (bq, W))
        m_old = m_ref[...]
        m_new = jnp.maximum(m_old, row_max)
        alpha = jnp.exp2(m_old - m_new)
        m_ref[...] = m_new
        p = [jnp.exp2(s - m_new) for s in tiles]
        l_ref[...] = alpha * l_ref[...] + functools.reduce(lambda a, b: a + b, p)
        pv = jnp.dot(jnp.concatenate([x.astype(BF16) for x in p], axis=1),
                     v_ref[pl.ds(off, bk), :], preferred_element_type=F32)
        acc_ref[...] = alpha * acc_ref[...] + pv

    n_diag = bq // bk
    assert n_diag % 2 == 0 and dh == W

    def full_steps(kj0, n):
        for d in range(n):
            step(kj0 + d, d % 2, False, False)

    def quad(jj, carry):
        full_steps(4 * jj, 4)
        return carry

    n_pairs = qi * (n_diag // 2)
    lax.fori_loop(0, n_pairs // 2, quad, 0)

    @pl.when(n_pairs % 2 == 1)
    def _():
        full_steps(2 * (n_pairs - 1), 2)

    for dj in range(n_diag):
        step(qi * n_diag + dj, dj % 2, True, dj == n_diag - 1)
    l = jnp.sum(l_ref[...], axis=1, keepdims=True)
    o_ref[...] = (acc_ref[...] / l).astype(o_ref.dtype)


def _fox_attention(q, k, v, caug, n_heads, bq=1024, bk=512):
    L, D = q.shape
    dh = D // n_heads
    vmem = (2 * (3 * L * dh * 2 + 2 * bq * dh * 2) + 3 * bq * dh * 4 + 2 * bq * bk * 4
            + 4 * bq * bk * 4)
    return pl.pallas_call(
        functools.partial(_attn_kernel, bq=bq, bk=bk, dh=dh),
        grid=(n_heads, L // bq),
        in_specs=[pl.BlockSpec((bq, dh), lambda h, i: (i, h)),
                  pl.BlockSpec((L, dh), lambda h, i: (0, h)),
                  pl.BlockSpec((L, dh), lambda h, i: (0, h)),
                  pl.BlockSpec((L, dh), lambda h, i: (0, h))],
        out_specs=pl.BlockSpec((bq, dh), lambda h, i: (i, h)),
        out_shape=jax.ShapeDtypeStruct((L, D), BF16),
        scratch_shapes=[pltpu.VMEM((bq, dh), F32), pltpu.VMEM((bq, V7X_LANES), F32),
                        pltpu.VMEM((bq, V7X_LANES), F32),
                        pltpu.VMEM((2, bq, bk), F32)],
        compiler_params=_cparams(("arbitrary", "arbitrary"), vmem),
        name="fox_attention",
    )(q, k, caug, v)


def kernel(x, a_norm, a_w_in, a_lambda_re, a_lambda_im, a_log_step, a_b_re, a_b_im, a_c_re, a_c_im, a_d, a_w_glu, kv_norm, w_k, w_v, w_f, b_f, b_norm, b_w_q, b_w_o, ffn_norm, ffn_w_up, ffn_conv_w, ffn_conv_b, ffn_w_down, final_norm):
    bsz, seq, d_model = x.shape
    n_a = a_w_in.shape[0]
    n_b = b_w_q.shape[0]
    n_heads = w_f.shape[1]
    dh = d_model // n_heads
    assert bsz == 1 and n_b >= 1
    xs = x.reshape(seq, d_model).astype(F32)
    w_up = _cast_bf16(ffn_w_up)
    w_down = _cast_bf16(ffn_w_down)
    w_glu = _cast_bf16(a_w_glu)

    def ffn(xs, li, final_g=None):
        return _ffn(xs, li, ffn_norm, w_up, ffn_conv_w, ffn_conv_b, w_down,
                    final_g=final_g)

    for i in range(n_a):
        h = _rmsnorm(xs, a_norm[i], BF16, tm=S5_ROW_BLOCK, segment_rows=S5_SEG)
        prep = _s5_prepare(a_lambda_re[i], a_lambda_im[i], a_log_step[i],
                           a_b_re[i], a_b_im[i], a_c_re[i], a_c_im[i], a_d[i])
        y = _s5_core(h, a_w_in, i, *prep)
        xs = _glu_residual(y, w_glu, i, xs)
        xs = ffn(xs, i)

    h_kv = _rmsnorm(xs, kv_norm, BF16)
    k = _matmul(h_kv, w_k, 0, BF16, name="k_proj")
    v = _matmul(h_kv, w_v, 0, BF16, name="v_proj")
    pieces = _cumsum_pieces(_fgate_logf(h_kv, w_f, b_f))
    caug = jnp.stack([p[:n_heads] for p in pieces], axis=-1)
    caug = jnp.pad(caug.transpose(1, 0, 2), ((0, 0), (0, 0), (0, dh - CK_PIECES)))
    caug = caug.reshape(seq, d_model)

    q_scale = dh ** -0.5 * LOG2E
    for j in range(n_b):
        li = n_a + j
        h = _rmsnorm(xs, b_norm[j], BF16)
        q = _matmul(h, b_w_q, j, BF16, out_scale=q_scale, name="q_proj")
        o = _fox_attention(q, k, v, caug, n_heads)
        xs = _matmul(o, b_w_o, j, F32, residual=xs, name="o_proj")
        xs = ffn(xs, li, final_g=final_norm if j == n_b - 1 else None)

    return xs.reshape(bsz, seq, d_model).astype(x.dtype)
```

```python
import functools
import math

import jax
import jax.numpy as jnp
from jax import lax
from jax.experimental import pallas as pl
from jax.experimental.pallas import tpu as pltpu

F32 = jnp.float32
BF16 = jnp.bfloat16

NORM_EPS = 1e-6
CONV_WIDTH = 3
LOG2E = math.log2(math.e)

V7X_SUBLANES = 8
V7X_LANES = 128
V7X_VMEM_LIMIT_BYTES = 56 * 1024 * 1024

S5_GROUPS_PER_BLOCK = 16
S5_ROW_BLOCK = 512
S5_SEG = S5_ROW_BLOCK // V7X_SUBLANES
S5_CHAINS = 2
FFN_ROW_CHUNK = 128
FFN_COL_GROUP = 256
CK_PIECES = 3
ATTN_KEYS_PER_ITER = 2048
MASK_VALUE = -1e30


def _cparams(semantics, vmem_bytes):
    budget = int(vmem_bytes) * 5 // 4 + (4 << 20)
    return pltpu.CompilerParams(
        dimension_semantics=semantics,
        vmem_limit_bytes=min(budget, V7X_VMEM_LIMIT_BYTES))


def _gelu_tanh(x):
    c = math.sqrt(2.0 / math.pi)
    return x * (0.5 * (1.0 + jnp.tanh(c * (x + 0.044715 * (x * x * x)))))


def _rms_normalize(x, g):
    ms = jnp.mean(x * x, axis=-1, keepdims=True)
    return x * lax.rsqrt(ms + NORM_EPS) * g


def _cmul(ar, ai, br, bi):
    return ar * br - ai * bi, ar * bi + ai * br


def _stacked(w):
    return w if w.ndim == 3 else w[None]


def _cast_kernel(w_ref, o_ref):
    o_ref[...] = w_ref[...].astype(o_ref.dtype)


def _cast_bf16(w, tr=512, max_cols=4096):
    shape = w.shape
    C = shape[-1]
    R = math.prod(shape[:-1])
    tc = C
    while tc > max_cols:
        tc //= 2
    assert C % tc == 0 and tc % V7X_LANES == 0 and R % tr == 0
    out = pl.pallas_call(
        _cast_kernel,
        grid=(R // tr, C // tc),
        in_specs=[pl.BlockSpec((tr, tc), lambda i, j: (i, j))],
        out_specs=pl.BlockSpec((tr, tc), lambda i, j: (i, j)),
        out_shape=jax.ShapeDtypeStruct((R, C), BF16),
        compiler_params=_cparams(("arbitrary", "arbitrary"), 2 * tr * tc * 6),
        name="cast_bf16",
    )(w.reshape(R, C))
    return out.reshape(shape)


def _rmsnorm_kernel(x_ref, g_ref, *rest):
    o_ref = rest[-1]
    h = _rms_normalize(x_ref[...], g_ref[...]).astype(o_ref.dtype)
    if len(rest) == 2:
        h = jnp.dot(rest[0][...], h, preferred_element_type=F32).astype(o_ref.dtype)
    o_ref[...] = h


def _segment_interleave(tm, segment_rows):
    n_seg = tm // segment_rows
    r = jnp.arange(tm)
    src = (r % n_seg) * segment_rows + r // n_seg
    return (src[:, None] == jnp.arange(tm)[None, :]).astype(BF16)


def _rmsnorm(x, g, out_dtype, tm=512, segment_rows=None):
    L, D = x.shape
    in_specs = [pl.BlockSpec((tm, D), lambda i: (i, 0)),
                pl.BlockSpec((1, D), lambda i: (0, 0))]
    args = [x, g.reshape(1, D)]
    if segment_rows is not None:
        assert out_dtype == BF16
        in_specs.append(pl.BlockSpec((tm, tm), lambda i: (0, 0)))
        args.append(_segment_interleave(tm, segment_rows))
    return pl.pallas_call(
        _rmsnorm_kernel,
        grid=(L // tm,),
        in_specs=in_specs,
        out_specs=pl.BlockSpec((tm, D), lambda i: (i, 0)),
        out_shape=jax.ShapeDtypeStruct((L, D), out_dtype),
        compiler_params=_cparams(("arbitrary",), 4 * tm * D * 8),
        name="rmsnorm",
    )(*args)


def _matmul_kernel(a_ref, b_ref, o_ref, *, out_scale):
    acc = jnp.dot(a_ref[...], b_ref[...].astype(BF16), preferred_element_type=F32)
    if out_scale is not None:
        acc = acc * out_scale
    o_ref[...] = acc.astype(o_ref.dtype)


def _matmul_res_kernel(a_ref, b_ref, r_ref, o_ref):
    o_ref[...] = r_ref[...] + jnp.dot(a_ref[...], b_ref[...].astype(BF16),
                                      preferred_element_type=F32)


def _matmul(a, w, layer, out_dtype, residual=None, out_scale=None, tm=2048, tn=512,
            name="matmul"):
    L, K = a.shape
    w = _stacked(w)
    N = w.shape[2]
    in_specs = [pl.BlockSpec((tm, K), lambda i, j: (i, 0)),
                pl.BlockSpec((None, K, tn), lambda i, j: (layer, 0, j))]
    args = [a, w]
    if residual is not None:
        in_specs.append(pl.BlockSpec((tm, tn), lambda i, j: (i, j)))
        args.append(residual)
        kern = _matmul_res_kernel
    else:
        kern = functools.partial(_matmul_kernel, out_scale=out_scale)
    vmem = 2 * (tm * K * 2 + K * tn * 4 + 2 * tm * tn * 4) + K * tn * 2 + 2 * tm * tn * 4
    return pl.pallas_call(
        kern,
        grid=(L // tm, N // tn),
        in_specs=in_specs,
        out_specs=pl.BlockSpec((tm, tn), lambda i, j: (i, j)),
        out_shape=jax.ShapeDtypeStruct((L, N), out_dtype),
        compiler_params=_cparams(("arbitrary", "arbitrary"), vmem),
        name=name,
    )(*args)


def _cpowers(br, bi, n):
    pr, pi = br[None], bi[None]
    k = 1
    while k < n:
        tr, ti = _cmul(pr, pi, pr[k - 1:k], pi[k - 1:k])
        pr = jnp.concatenate([pr, tr], axis=0)
        pi = jnp.concatenate([pi, ti], axis=0)
        k *= 2
    return pr, pi


def _s5_prepare(lam_re, lam_im, log_step, b_re, b_im, c_re, c_im, d):
    G, P = lam_re.shape
    C = b_re.shape[-1]
    gb = S5_GROUPS_PER_BLOCK
    nb = G // gb
    S = V7X_SUBLANES
    lam_re = lam_re.astype(F32)
    lam_im = lam_im.astype(F32)
    dt = jnp.exp(log_step.astype(F32))[:, None]
    mag = jnp.exp(lam_re * dt)
    lbr = mag * jnp.cos(lam_im * dt)
    lbi = mag * jnp.sin(lam_im * dt)
    den = lam_re * lam_re + lam_im * lam_im
    fr = ((lbr - 1.0) * lam_re + lbi * lam_im) / den
    fi = (lbi * lam_re - (lbr - 1.0) * lam_im) / den
    bbr, bbi = _cmul(fr[..., None], fi[..., None], b_re.astype(F32), b_im.astype(F32))
    eye = jnp.eye(gb, dtype=F32)

    def in_proj(m):
        m = m.reshape(nb, gb, P, C)
        return jnp.einsum('ngpc,gh->ngchp', m, eye).reshape(nb, gb * C, gb * P)

    def out_proj(m):
        m = m.reshape(nb, gb, C, P)
        return jnp.einsum('ngcp,gh->ngphc', m, eye).reshape(nb, gb * P, gb * C)

    wb = jnp.concatenate([in_proj(bbr), in_proj(bbi)], axis=-1).astype(BF16)
    wc = jnp.concatenate([out_proj(c_re.astype(F32)), out_proj(-c_im.astype(F32))],
                         axis=-2).astype(BF16)

    def lanes(m):
        return m.reshape(m.shape[0], nb, gb * P).transpose(1, 0, 2)

    tr, ti = _cpowers(lbr, lbi, S5_SEG)
    tab = jnp.concatenate([lanes(tr), lanes(ti)], axis=-1)
    sr, si = _cpowers(tr[-1], ti[-1], S)
    rows = jnp.arange(S)[:, None, None]
    kinds = [jnp.broadcast_to(lbr[None], (S, G, P)),
             jnp.broadcast_to(lbi[None], (S, G, P))]
    for k in (1, 2, 4):
        kinds += [jnp.where(rows >= k, sr[k - 1][None], 0.0),
                  jnp.where(rows >= k, si[k - 1][None], 0.0)]
    kinds += [sr, si]
    cst = jnp.stack([lanes(m) for m in kinds], axis=1)
    dvec = d.astype(F32).reshape(nb, 1, gb * C)
    return wb, wc, cst, tab, dvec


def _s5_unpermutation(tm):
    l = jnp.arange(tm)
    src = (l % S5_SEG) * V7X_SUBLANES + l // S5_SEG
    return (src[:, None] == jnp.arange(tm)[None, :]).astype(BF16)


def _s5_kernel(h_ref, win_ref, wb_ref, wc_ref, cst_ref, tab_ref, d_ref,
               unperm_ref, o_ref, st_ref, carry_ref, *, tm, ns, cb):
    S = V7X_SUBLANES
    n_tiles = tm // S

    @pl.when(pl.program_id(1) == 0)
    def _():
        carry_ref[...] = jnp.zeros_like(carry_ref)

    h = h_ref[...]

    def project(c):
        u = jnp.dot(h, win_ref[:, c * cb:(c + 1) * cb].astype(BF16),
                    preferred_element_type=F32)
        st_ref[c] = jnp.dot(u.astype(BF16), wb_ref[c],
                            preferred_element_type=F32)
        return u

    def scan(c):
        ar = cst_ref[c, 0]
        ai = cst_ref[c, 1]
        dr = jnp.zeros((S, ns), F32)
        di = jnp.zeros((S, ns), F32)
        for t in range(n_tiles):
            rows = slice(t * S, (t + 1) * S)
            dr, di = (ar * dr - ai * di + st_ref[c, rows, 0:ns],
                      ar * di + ai * dr + st_ref[c, rows, ns:2 * ns])
            st_ref[c, rows, 0:ns] = dr
            st_ref[c, rows, ns:2 * ns] = di
        for lvl, k in enumerate((1, 2, 4)):
            br = cst_ref[c, 2 + 2 * lvl]
            bi = cst_ref[c, 3 + 2 * lvl]
            sr = pltpu.roll(dr, k, 0)
            si = pltpu.roll(di, k, 0)
            dr, di = dr + br * sr - bi * si, di + br * si + bi * sr
        cin_r = carry_ref[c, 0:1, :]
        cin_i = carry_ref[c, 1:2, :]
        pr = cst_ref[c, 8]
        pi = cst_ref[c, 9]
        dr, di = dr + pr * cin_r - pi * cin_i, di + pr * cin_i + pi * cin_r
        first = lax.broadcasted_iota(jnp.int32, (S, ns), 0) == 0
        cr = jnp.where(first, cin_r, pltpu.roll(dr, 1, 0))
        ci = jnp.where(first, cin_i, pltpu.roll(di, 1, 0))
        carry_ref[c, 0:1, :] = dr[S - 1:S, :]
        carry_ref[c, 1:2, :] = di[S - 1:S, :]
        for t in range(n_tiles):
            rows = slice(t * S, (t + 1) * S)
            tr = tab_ref[c, t:t + 1, 0:ns]
            ti = tab_ref[c, t:t + 1, ns:2 * ns]
            st_ref[c, rows, 0:ns] += tr * cr - ti * ci
            st_ref[c, rows, ns:2 * ns] += tr * ci + ti * cr

    def emit(c, u):
        y = jnp.dot(st_ref[c].astype(BF16), wc_ref[c],
                    preferred_element_type=F32) + d_ref[c] * u
        act = _gelu_tanh(y).astype(BF16)
        o_ref[:, c * cb:(c + 1) * cb] = jnp.dot(
            unperm_ref[...], act, preferred_element_type=F32).astype(o_ref.dtype)

    us = [project(c) for c in range(S5_CHAINS)]
    for c in range(S5_CHAINS):
        scan(c)
        emit(c, us[c])


def _s5_core(h, w_in, layer, wb, wc, cst, tab, dvec):
    L, D = h.shape
    tm = S5_ROW_BLOCK
    nc = S5_CHAINS
    nb, cb, ns2 = wb.shape
    ns = ns2 // 2
    vmem = (2 * (tm * D * 2 + nc * (D * cb * 4 + 2 * cb * ns2 * 2 + 10 * 8 * ns * 4
                                    + S5_SEG * ns2 * 4 + tm * cb * 2) + tm * tm * 2)
            + nc * (tm * ns2 * 4 + D * cb * 2 + 4 * tm * cb * 4) + tm * ns2 * 4)
    return pl.pallas_call(
        functools.partial(_s5_kernel, tm=tm, ns=ns, cb=cb),
        grid=(nb // nc, L // tm),
        in_specs=[
            pl.BlockSpec((tm, D), lambda j, i: (i, 0)),
            pl.BlockSpec((None, D, nc * cb), lambda j, i: (layer, 0, j)),
            pl.BlockSpec((nc, cb, ns2), lambda j, i: (j, 0, 0)),
            pl.BlockSpec((nc, ns2, cb), lambda j, i: (j, 0, 0)),
            pl.BlockSpec((nc, 10, V7X_SUBLANES, ns), lambda j, i: (j, 0, 0, 0)),
            pl.BlockSpec((nc, S5_SEG, ns2), lambda j, i: (j, 0, 0)),
            pl.BlockSpec((nc, 1, cb), lambda j, i: (j, 0, 0)),
            pl.BlockSpec((tm, tm), lambda j, i: (0, 0)),
        ],
        out_specs=pl.BlockSpec((tm, nc * cb), lambda j, i: (i, j)),
        out_shape=jax.ShapeDtypeStruct((L, D), BF16),
        scratch_shapes=[pltpu.VMEM((nc, tm, ns2), F32),
                        pltpu.VMEM((nc, V7X_SUBLANES, ns), F32)],
        compiler_params=_cparams(("arbitrary", "arbitrary"), vmem),
        name="s5_core",
    )(h, w_in, wb, wc, cst, tab, dvec, _s5_unpermutation(tm))


def _glu_kernel(y_ref, wv_ref, wg_ref, x_ref, o_ref):
    y = y_ref[...]
    val = jnp.dot(y, wv_ref[...], preferred_element_type=F32)
    gate = jnp.dot(y, wg_ref[...], preferred_element_type=F32)
    o_ref[...] = x_ref[...] + val * (1.0 / (1.0 + jnp.exp(-gate)))


def _glu_residual(y, w_glu, layer, x, tm=2048, tn=256):
    L, D = y.shape
    nj = D // tn
    vmem = 2 * (tm * D * 2 + 2 * D * tn * 2 + 2 * tm * tn * 4) + 4 * tm * tn * 4
    return pl.pallas_call(
        _glu_kernel,
        grid=(L // tm, nj),
        in_specs=[pl.BlockSpec((tm, D), lambda i, j: (i, 0)),
                  pl.BlockSpec((None, D, tn), lambda i, j: (layer, 0, j)),
                  pl.BlockSpec((None, D, tn), lambda i, j: (layer, 0, nj + j)),
                  pl.BlockSpec((tm, tn), lambda i, j: (i, j))],
        out_specs=pl.BlockSpec((tm, tn), lambda i, j: (i, j)),
        out_shape=jax.ShapeDtypeStruct((L, D), F32),
        compiler_params=_cparams(("arbitrary", "arbitrary"), vmem),
        name="s5_glu",
    )(y, w_glu, w_glu, x)


def _ffn_kernel(x_ref, g_ref, wg_ref, wv_ref, cw_ref, cb_ref, wd_ref, *rest,
                tm, tf, final_norm):
    if final_norm:
        fg_ref, o_ref, h_ref, halo_ref, gs_ref = rest
    else:
        o_ref, h_ref, halo_ref, gs_ref = rest
    i = pl.program_id(0)
    f = pl.program_id(1)
    H = V7X_SUBLANES

    def row_chunks(fn):
        def body(c, carry):
            fn(pl.ds(pl.multiple_of(c * FFN_ROW_CHUNK, FFN_ROW_CHUNK), FFN_ROW_CHUNK))
            return carry
        lax.fori_loop(0, tm // FFN_ROW_CHUNK, body, 0)

    @pl.when(f == 0)
    def _():
        def normalize(rows):
            h_ref[rows, :] = _rms_normalize(x_ref[rows, :], g_ref[...]).astype(BF16)
            o_ref[rows, :] = jnp.zeros((FFN_ROW_CHUNK, o_ref.shape[1]), F32)
        row_chunks(normalize)

    h = h_ref[...]

    def up(cols):
        gate = jnp.dot(h, wg_ref[:, cols], preferred_element_type=F32)
        val = jnp.dot(h, wv_ref[:, cols], preferred_element_type=F32)
        return gate, val

    def activate(cols, gate, val):
        prev = halo_ref[f, :, cols]
        gs_ref[0:H, cols] = jnp.where(i == 0, jnp.zeros_like(prev), prev)
        gs_ref[H:H + tm, cols] = gate
        halo_ref[f, :, cols] = gate[tm - H:, :]
        g1 = gs_ref[H - 1:H - 1 + tm, cols]
        g2 = gs_ref[H - 2:H - 2 + tm, cols]
        conv = (cw_ref[0:1, cols] * g2 + cw_ref[1:2, cols] * g1
                + cw_ref[2:3, cols] * gate + cb_ref[:, cols])
        return (_gelu_tanh(conv) * val).astype(BF16)

    groups = [slice(c, c + FFN_COL_GROUP) for c in range(0, tf, FFN_COL_GROUP)]
    ups = [up(cols) for cols in groups]
    for cols, (gate, val) in zip(groups, ups):
        act = activate(cols, gate, val)
        o_ref[...] += jnp.dot(act, wd_ref[cols, :], preferred_element_type=F32)

    @pl.when(f == pl.num_programs(1) - 1)
    def _():
        def finish(rows):
            y = x_ref[rows, :] + o_ref[rows, :]
            if final_norm:
                y = _rms_normalize(y, fg_ref[...])
            o_ref[rows, :] = y
        row_chunks(finish)


def _ffn(x, layer, ffn_norm, w_up, conv_w, conv_b, w_down, final_g=None,
         tm=1024, tf=512):
    L, D = x.shape
    n_layers, F, _ = w_down.shape
    nf = F // tf
    final_norm = final_g is not None
    in_specs = [
        pl.BlockSpec((tm, D), lambda i, f: (i, 0), pipeline_mode=pl.Buffered(1)),
        pl.BlockSpec((None, 1, D), lambda i, f: (layer, 0, 0)),
        pl.BlockSpec((None, D, tf), lambda i, f: (layer, 0, f)),
        pl.BlockSpec((None, D, tf), lambda i, f: (layer, 0, nf + f)),
        pl.BlockSpec((None, CONV_WIDTH, tf), lambda i, f: (layer, 0, f)),
        pl.BlockSpec((None, 1, tf), lambda i, f: (layer, 0, f)),
        pl.BlockSpec((None, tf, D), lambda i, f: (layer, f, 0)),
    ]
    args = [x, ffn_norm.reshape(n_layers, 1, D), w_up, w_up, conv_w,
            conv_b.reshape(n_layers, 1, F), w_down]
    if final_norm:
        in_specs.append(pl.BlockSpec((1, D), lambda i, f: (0, 0)))
        args.append(final_g.reshape(1, D))
    vmem = (tm * D * 4 + 2 * tm * D * 4 + 2 * 3 * D * tf * 2 + tm * D * 2
            + nf * V7X_SUBLANES * tf * 4 + (tm + 8) * tf * 4 + 6 * tm * tf * 4
            + tm * D * 4)
    return pl.pallas_call(
        functools.partial(_ffn_kernel, tm=tm, tf=tf, final_norm=final_norm),
        grid=(L // tm, nf),
        in_specs=in_specs,
        out_specs=pl.BlockSpec((tm, D), lambda i, f: (i, 0)),
        out_shape=jax.ShapeDtypeStruct((L, D), F32),
        scratch_shapes=[pltpu.VMEM((tm, D), BF16),
                        pltpu.VMEM((nf, V7X_SUBLANES, tf), F32),
                        pltpu.VMEM((tm + V7X_SUBLANES, tf), F32)],
        compiler_params=_cparams(("arbitrary", "arbitrary"), vmem),
        name="conv_glu_ffn",
    )(*args)


def _fgate_kernel(wft_ref, h_ref, bf_ref, o_ref):
    z = lax.dot_general(wft_ref[...], h_ref[...], (((1,), (1,)), ((), ())),
                        preferred_element_type=F32)
    z = z + bf_ref[...]
    o_ref[...] = jnp.minimum(z, 0.0) - jnp.log1p(jnp.exp(-jnp.abs(z)))


def _fgate_logf(h_kv, w_f, b_f, tm=1024):
    L, D = h_kv.shape
    H = w_f.shape[1]
    hp = V7X_LANES
    wft = jnp.zeros((hp, D), BF16).at[:H].set(w_f.T.astype(BF16))
    bfp = jnp.zeros((hp, 1), F32).at[:H, 0].set(b_f.astype(F32))
    return pl.pallas_call(
        _fgate_kernel,
        grid=(L // tm,),
        in_specs=[pl.BlockSpec((hp, D), lambda i: (0, 0)),
                  pl.BlockSpec((tm, D), lambda i: (i, 0)),
                  pl.BlockSpec((hp, 1), lambda i: (0, 0))],
        out_specs=pl.BlockSpec((hp, tm), lambda i: (0, i)),
        out_shape=jax.ShapeDtypeStruct((hp, L), F32),
        compiler_params=_cparams(("arbitrary",), 2 * (tm * D * 2 + hp * D * 2)
                                 + 8 * hp * tm * 4),
        name="fgate_logf",
    )(wft, h_kv, bfp)


def _cumsum_kernel(x_ref, *o_refs, tb):
    r = lax.broadcasted_iota(jnp.int32, (tb, tb), 0)
    c = lax.broadcasted_iota(jnp.int32, (tb, tb), 1)
    tri = (r <= c).astype(F32)

    def body(b, carry):
        off = pl.multiple_of(b * tb, tb)
        cs = jnp.dot(x_ref[:, pl.ds(off, tb)], tri,
                     precision=lax.Precision.HIGHEST,
                     preferred_element_type=F32) + carry
        rem = cs * (-LOG2E)
        for o_ref in o_refs:
            piece = rem.astype(BF16)
            o_ref[:, pl.ds(off, tb)] = piece
            rem = rem - piece.astype(F32)
        return cs[:, tb - 1:tb]

    lax.fori_loop(0, x_ref.shape[1] // tb, body,
                  jnp.zeros((x_ref.shape[0], 1), F32))


def _cumsum_pieces(x, tb=256):
    R, L = x.shape
    return pl.pallas_call(
        functools.partial(_cumsum_kernel, tb=tb),
        out_shape=[jax.ShapeDtypeStruct((R, L), BF16)] * CK_PIECES,
        compiler_params=pltpu.CompilerParams(
            vmem_limit_bytes=min(8 * R * L * 4, V7X_VMEM_LIMIT_BYTES)),
        name="logf_cumsum",
    )(x)


def _attn_kernel(q_ref, k_ref, c_ref, v_ref, o_ref, acc_ref, m_ref, l_ref, s_ref,
                 *, bq, bk, dh, n_heads):
    head = pl.program_id(0)
    qi = pl.program_id(1)
    W = V7X_LANES
    n_lane_tiles = bk // W
    lane = lax.broadcasted_iota(jnp.int32, (bq, dh), 1)
    pick = (lane % n_heads == head) & (lane < CK_PIECES * n_heads)
    qa = jnp.concatenate([q_ref[...], pick.astype(BF16)], axis=1)
    acc_ref[...] = jnp.zeros_like(acc_ref)
    m_ref[...] = jnp.full_like(m_ref, MASK_VALUE)
    l_ref[...] = jnp.zeros_like(l_ref)

    def scores(kj):
        off = pl.multiple_of(kj * bk, bk)
        ka = jnp.concatenate([k_ref[pl.ds(off, bk), :], c_ref[pl.ds(off, bk), :]],
                             axis=1)
        return lax.dot_general(qa, ka, (((1,), (1,)), ((), ())),
                               preferred_element_type=F32)

    s_ref[0] = scores(0)

    def step(kj, slot, masked, last):
        off = pl.multiple_of(kj * bk, bk)
        if not last:
            s_ref[1 - slot] = scores(kj + 1)
        tiles = []
        for t in range(n_lane_tiles):
            s = s_ref[slot, :, t * W:(t + 1) * W]
            if masked:
                key = off + t * W + lax.broadcasted_iota(jnp.int32, (bq, W), 1)
                qry = qi * bq + lax.broadcasted_iota(jnp.int32, (bq, W), 0)
                s = jnp.where(key <= qry, s, MASK_VALUE)
            tiles.append(s)
        tile_max = functools.reduce(jnp.maximum, tiles)
        row_max = jnp.broadcast_to(jnp.max(tile_max, axis=1, keepdims=True), (bq, W))
        m_old = m_ref[...]
        m_new = jnp.maximum(m_old, row_max)
        alpha = jnp.exp2(m_old - m_new)
        m_ref[...] = m_new
        p = [jnp.exp2(s - m_new) for s in tiles]
        l_ref[...] = alpha * l_ref[...] + functools.reduce(lambda a, b: a + b, p)
        pv = jnp.dot(jnp.concatenate([x.astype(BF16) for x in p], axis=1),
                     v_ref[pl.ds(off, bk), :], preferred_element_type=F32)
        acc_ref[...] = alpha * acc_ref[...] + pv

    n_diag = bq // bk
    unroll = max(2, ATTN_KEYS_PER_ITER // bk)
    assert dh == W and unroll % 2 == 0 and unroll % n_diag == 0

    def full_steps(kj0, n):
        for d in range(n):
            step(kj0 + d, d % 2, False, False)

    def body(jj, carry):
        full_steps(unroll * jj, unroll)
        return carry

    n_full = qi * n_diag
    lax.fori_loop(0, n_full // unroll, body, 0)
    rem = n_full % unroll
    for r in range(0, unroll, n_diag):
        @pl.when(rem == r)
        def _(r=r):
            full_steps(n_full - r, r)
            for dj in range(n_diag):
                step(n_full + dj, (r + dj) % 2, True, dj == n_diag - 1)

    l = jnp.sum(l_ref[...], axis=1, keepdims=True)
    o_ref[...] = (acc_ref[...] / l).astype(o_ref.dtype)


def _fox_attention(q, k, v, caug, n_heads, bq=1024, bk=1024):
    L, D = q.shape
    dh = D // n_heads
    assert CK_PIECES * n_heads <= dh
    vmem = (2 * (3 * L * dh * 2 + 2 * bq * dh * 2) + 3 * bq * dh * 4 + 2 * bq * bk * 4
            + 4 * bq * bk * 4)
    return pl.pallas_call(
        functools.partial(_attn_kernel, bq=bq, bk=bk, dh=dh, n_heads=n_heads),
        grid=(n_heads, L // bq),
        in_specs=[pl.BlockSpec((bq, dh), lambda h, i: (i, h)),
                  pl.BlockSpec((L, dh), lambda h, i: (0, h)),
                  pl.BlockSpec((L, dh), lambda h, i: (0, 0)),
                  pl.BlockSpec((L, dh), lambda h, i: (0, h))],
        out_specs=pl.BlockSpec((bq, dh), lambda h, i: (i, h)),
        out_shape=jax.ShapeDtypeStruct((L, D), BF16),
        scratch_shapes=[pltpu.VMEM((bq, dh), F32), pltpu.VMEM((bq, V7X_LANES), F32),
                        pltpu.VMEM((bq, V7X_LANES), F32),
                        pltpu.VMEM((2, bq, bk), F32)],
        compiler_params=_cparams(("arbitrary", "arbitrary"), vmem),
        name="fox_attention",
    )(q, k, caug, v)


def kernel(x, a_norm, a_w_in, a_lambda_re, a_lambda_im, a_log_step, a_b_re, a_b_im, a_c_re, a_c_im, a_d, a_w_glu, kv_norm, w_k, w_v, w_f, b_f, b_norm, b_w_q, b_w_o, ffn_norm, ffn_w_up, ffn_conv_w, ffn_conv_b, ffn_w_down, final_norm):
    bsz, seq, d_model = x.shape
    n_a = a_w_in.shape[0]
    n_b = b_w_q.shape[0]
    n_heads = w_f.shape[1]
    dh = d_model // n_heads
    assert bsz == 1 and n_b >= 1
    xs = x.reshape(seq, d_model).astype(F32)
    w_up = _cast_bf16(ffn_w_up)
    w_down = _cast_bf16(ffn_w_down)
    w_glu = _cast_bf16(a_w_glu)

    def ffn(xs, li, final_g=None):
        return _ffn(xs, li, ffn_norm, w_up, ffn_conv_w, ffn_conv_b, w_down,
                    final_g=final_g)

    for i in range(n_a):
        h = _rmsnorm(xs, a_norm[i], BF16, tm=S5_ROW_BLOCK, segment_rows=S5_SEG)
        prep = _s5_prepare(a_lambda_re[i], a_lambda_im[i], a_log_step[i],
                           a_b_re[i], a_b_im[i], a_c_re[i], a_c_im[i], a_d[i])
        y = _s5_core(h, a_w_in, i, *prep)
        xs = _glu_residual(y, w_glu, i, xs)
        xs = ffn(xs, i)

    h_kv = _rmsnorm(xs, kv_norm, BF16)
    k = _matmul(h_kv, w_k, 0, BF16, name="k_proj")
    v = _matmul(h_kv, w_v, 0, BF16, name="v_proj")
    pieces = _cumsum_pieces(_fgate_logf(h_kv, w_f, b_f))
    caug = jnp.concatenate([p[:n_heads] for p in pieces], axis=0).T
    caug = jnp.pad(caug, ((0, 0), (0, dh - CK_PIECES * n_heads)))

    q_scale = dh ** -0.5 * LOG2E
    for j in range(n_b):
        li = n_a + j
        h = _rmsnorm(xs, b_norm[j], BF16)
        q = _matmul(h, b_w_q, j, BF16, out_scale=q_scale, name="q_proj")
        o = _fox_attention(q, k, v, caug, n_heads)
        xs = _matmul(o, b_w_o, j, F32, residual=xs, name="o_proj")
        xs = ffn(xs, li, final_g=final_norm if j == n_b - 1 else None)

    return xs.reshape(bsz, seq, d_model).astype(x.dtype)
```

```python
import functools
import math

import jax
import jax.numpy as jnp
from jax import lax
from jax.experimental import pallas as pl
from jax.experimental.pallas import tpu as pltpu

F32 = jnp.float32
BF16 = jnp.bfloat16

NORM_EPS = 1e-6
CONV_WIDTH = 3
LOG2E = math.log2(math.e)

V7X_SUBLANES = 8
V7X_LANES = 128
V7X_VMEM_LIMIT_BYTES = 58 * 1024 * 1024

S5_GROUPS_PER_BLOCK = 16
S5_ROW_BLOCK = 512
S5_SEG = S5_ROW_BLOCK // V7X_SUBLANES
S5_CHAINS = 2
FFN_ROW_CHUNK = 128
FFN_COL_GROUP = 256
CK_PIECES = 3
ATTN_KEYS_PER_ITER = 2048
MASK_VALUE = -1e30


def _cparams(semantics, vmem_bytes):
    budget = int(vmem_bytes) * 5 // 4 + (4 << 20)
    return pltpu.CompilerParams(
        dimension_semantics=semantics,
        vmem_limit_bytes=min(budget, V7X_VMEM_LIMIT_BYTES))


def _gelu_tanh(x):
    c = math.sqrt(2.0 / math.pi)
    return x * (0.5 * (1.0 + jnp.tanh(c * (x + 0.044715 * (x * x * x)))))


def _rms_normalize(x, g):
    ms = jnp.mean(x * x, axis=-1, keepdims=True)
    return x * lax.rsqrt(ms + NORM_EPS) * g


def _cmul(ar, ai, br, bi):
    return ar * br - ai * bi, ar * bi + ai * br


def _stacked(w):
    return w if w.ndim == 3 else w[None]


def _cast_kernel(w_ref, o_ref):
    o_ref[...] = w_ref[...].astype(o_ref.dtype)


def _cast_bf16(w, col_range=None, tr=512, max_cols=4096):
    shape = w.shape
    c0, c1 = col_range if col_range is not None else (0, shape[-1])
    C = c1 - c0
    R = math.prod(shape[:-1])
    tc = C
    while tc > max_cols:
        tc //= 2
    assert C % tc == 0 and c0 % tc == 0 and tc % V7X_LANES == 0 and R % tr == 0
    out = pl.pallas_call(
        _cast_kernel,
        grid=(R // tr, C // tc),
        in_specs=[pl.BlockSpec((tr, tc), lambda i, j: (i, j + c0 // tc))],
        out_specs=pl.BlockSpec((tr, tc), lambda i, j: (i, j)),
        out_shape=jax.ShapeDtypeStruct((R, C), BF16),
        compiler_params=_cparams(("arbitrary", "arbitrary"), 2 * tr * tc * 6),
        name="cast_bf16",
    )(w.reshape(R, shape[-1]))
    return out.reshape(shape[:-1] + (C,))


def _rmsnorm_kernel(x_ref, g_ref, *rest):
    o_ref = rest[-1]
    h = _rms_normalize(x_ref[...], g_ref[...]).astype(o_ref.dtype)
    if len(rest) == 2:
        h = jnp.dot(rest[0][...], h, preferred_element_type=F32).astype(o_ref.dtype)
    o_ref[...] = h


def _segment_interleave(tm, segment_rows):
    n_seg = tm // segment_rows
    r = jnp.arange(tm)
    src = (r % n_seg) * segment_rows + r // n_seg
    return (src[:, None] == jnp.arange(tm)[None, :]).astype(BF16)


def _rmsnorm(x, g, out_dtype, tm=512, segment_rows=None):
    L, D = x.shape
    in_specs = [pl.BlockSpec((tm, D), lambda i: (i, 0)),
                pl.BlockSpec((1, D), lambda i: (0, 0))]
    args = [x, g.reshape(1, D)]
    if segment_rows is not None:
        assert out_dtype == BF16
        in_specs.append(pl.BlockSpec((tm, tm), lambda i: (0, 0)))
        args.append(_segment_interleave(tm, segment_rows))
    return pl.pallas_call(
        _rmsnorm_kernel,
        grid=(L // tm,),
        in_specs=in_specs,
        out_specs=pl.BlockSpec((tm, D), lambda i: (i, 0)),
        out_shape=jax.ShapeDtypeStruct((L, D), out_dtype),
        compiler_params=_cparams(("arbitrary",), 4 * tm * D * 8),
        name="rmsnorm",
    )(*args)


def _matmul_kernel(a_ref, b_ref, o_ref, *, out_scale):
    acc = jnp.dot(a_ref[...], b_ref[...].astype(BF16), preferred_element_type=F32)
    if out_scale is not None:
        acc = acc * out_scale
    o_ref[...] = acc.astype(o_ref.dtype)


def _matmul_res_kernel(a_ref, b_ref, r_ref, o_ref):
    o_ref[...] = r_ref[...] + jnp.dot(a_ref[...], b_ref[...].astype(BF16),
                                      preferred_element_type=F32)


def _matmul(a, w, layer, out_dtype, residual=None, out_scale=None, tm=2048, tn=512,
            name="matmul"):
    L, K = a.shape
    w = _stacked(w)
    N = w.shape[2]
    in_specs = [pl.BlockSpec((tm, K), lambda i, j: (i, 0)),
                pl.BlockSpec((None, K, tn), lambda i, j: (layer, 0, j))]
    args = [a, w]
    if residual is not None:
        in_specs.append(pl.BlockSpec((tm, tn), lambda i, j: (i, j)))
        args.append(residual)
        kern = _matmul_res_kernel
    else:
        kern = functools.partial(_matmul_kernel, out_scale=out_scale)
    vmem = 2 * (tm * K * 2 + K * tn * 4 + 2 * tm * tn * 4) + K * tn * 2 + 2 * tm * tn * 4
    return pl.pallas_call(
        kern,
        grid=(L // tm, N // tn),
        in_specs=in_specs,
        out_specs=pl.BlockSpec((tm, tn), lambda i, j: (i, j)),
        out_shape=jax.ShapeDtypeStruct((L, N), out_dtype),
        compiler_params=_cparams(("arbitrary", "arbitrary"), vmem),
        name=name,
    )(*args)


def _cpowers(br, bi, n):
    pr, pi = br[None], bi[None]
    k = 1
    while k < n:
        tr, ti = _cmul(pr, pi, pr[k - 1:k], pi[k - 1:k])
        pr = jnp.concatenate([pr, tr], axis=0)
        pi = jnp.concatenate([pi, ti], axis=0)
        k *= 2
    return pr, pi


def _s5_prepare(lam_re, lam_im, log_step, b_re, b_im, c_re, c_im, d):
    G, P = lam_re.shape
    C = b_re.shape[-1]
    gb = S5_GROUPS_PER_BLOCK
    nb = G // gb
    S = V7X_SUBLANES
    lam_re = lam_re.astype(F32)
    lam_im = lam_im.astype(F32)
    dt = jnp.exp(log_step.astype(F32))[:, None]
    mag = jnp.exp(lam_re * dt)
    lbr = mag * jnp.cos(lam_im * dt)
    lbi = mag * jnp.sin(lam_im * dt)
    den = lam_re * lam_re + lam_im * lam_im
    fr = ((lbr - 1.0) * lam_re + lbi * lam_im) / den
    fi = (lbi * lam_re - (lbr - 1.0) * lam_im) / den
    bbr, bbi = _cmul(fr[..., None], fi[..., None], b_re.astype(F32), b_im.astype(F32))
    def block_diag(m):
        a, b = m.shape[1:]
        mt = m.reshape(nb, gb, a, b).transpose(0, 1, 3, 2)
        same = (jnp.arange(gb)[:, None, None]
                == (jnp.arange(gb * a) // a)[None, None, :])
        tiled = jnp.tile(mt, (1, 1, 1, gb))
        return jnp.where(same, tiled, 0.0).reshape(nb, gb * b, gb * a)

    wb = jnp.concatenate([block_diag(bbr), block_diag(bbi)], axis=-1).astype(BF16)
    wc = jnp.concatenate([block_diag(c_re.astype(F32)), block_diag(-c_im.astype(F32))],
                         axis=-2).astype(BF16)

    def lanes(m):
        return m.reshape(m.shape[0], nb, gb * P).transpose(1, 0, 2)

    tr, ti = _cpowers(lbr, lbi, S5_SEG)
    tab = jnp.concatenate([lanes(tr), lanes(ti)], axis=-1)
    sr, si = _cpowers(tr[-1], ti[-1], S)
    rows = jnp.arange(S)[:, None, None]
    kinds = [jnp.broadcast_to(lbr[None], (S, G, P)),
             jnp.broadcast_to(lbi[None], (S, G, P))]
    for k in (1, 2, 4):
        kinds += [jnp.where(rows >= k, sr[k - 1][None], 0.0),
                  jnp.where(rows >= k, si[k - 1][None], 0.0)]
    kinds += [sr, si]
    cst = jnp.stack([lanes(m) for m in kinds], axis=1)
    dvec = d.astype(F32).reshape(nb, 1, gb * C)
    return wb, wc, cst, tab, dvec


def _s5_unpermutation(tm):
    l = jnp.arange(tm)
    src = (l % S5_SEG) * V7X_SUBLANES + l // S5_SEG
    return (src[:, None] == jnp.arange(tm)[None, :]).astype(BF16)


def _s5_kernel(h_ref, win_ref, wb_ref, wc_ref, cst_ref, tab_ref, d_ref,
               unperm_ref, o_ref, st_ref, carry_ref, *, tm, ns, cb):
    S = V7X_SUBLANES
    n_tiles = tm // S

    @pl.when(pl.program_id(1) == 0)
    def _():
        carry_ref[...] = jnp.zeros_like(carry_ref)

    h = h_ref[...]

    def project(c):
        u = jnp.dot(h, win_ref[:, c * cb:(c + 1) * cb].astype(BF16),
                    preferred_element_type=F32)
        st_ref[c] = jnp.dot(u.astype(BF16), wb_ref[c],
                            preferred_element_type=F32)
        return u

    def scan(c):
        ar = cst_ref[c, 0]
        ai = cst_ref[c, 1]
        dr = jnp.zeros((S, ns), F32)
        di = jnp.zeros((S, ns), F32)
        for t in range(n_tiles):
            rows = slice(t * S, (t + 1) * S)
            dr, di = (ar * dr - ai * di + st_ref[c, rows, 0:ns],
                      ar * di + ai * dr + st_ref[c, rows, ns:2 * ns])
            st_ref[c, rows, 0:ns] = dr
            st_ref[c, rows, ns:2 * ns] = di
        for lvl, k in enumerate((1, 2, 4)):
            br = cst_ref[c, 2 + 2 * lvl]
            bi = cst_ref[c, 3 + 2 * lvl]
            sr = pltpu.roll(dr, k, 0)
            si = pltpu.roll(di, k, 0)
            dr, di = dr + br * sr - bi * si, di + br * si + bi * sr
        cin_r = carry_ref[c, 0:1, :]
        cin_i = carry_ref[c, 1:2, :]
        pr = cst_ref[c, 8]
        pi = cst_ref[c, 9]
        dr, di = dr + pr * cin_r - pi * cin_i, di + pr * cin_i + pi * cin_r
        first = lax.broadcasted_iota(jnp.int32, (S, ns), 0) == 0
        cr = jnp.where(first, cin_r, pltpu.roll(dr, 1, 0))
        ci = jnp.where(first, cin_i, pltpu.roll(di, 1, 0))
        carry_ref[c, 0:1, :] = dr[S - 1:S, :]
        carry_ref[c, 1:2, :] = di[S - 1:S, :]
        for t in range(n_tiles):
            rows = slice(t * S, (t + 1) * S)
            tr = tab_ref[c, t:t + 1, 0:ns]
            ti = tab_ref[c, t:t + 1, ns:2 * ns]
            st_ref[c, rows, 0:ns] += tr * cr - ti * ci
            st_ref[c, rows, ns:2 * ns] += tr * ci + ti * cr

    def emit(c, u):
        y = jnp.dot(st_ref[c].astype(BF16), wc_ref[c],
                    preferred_element_type=F32) + d_ref[c] * u
        act = _gelu_tanh(y).astype(BF16)
        o_ref[:, c * cb:(c + 1) * cb] = jnp.dot(
            unperm_ref[...], act, preferred_element_type=F32).astype(o_ref.dtype)

    us = [project(c) for c in range(S5_CHAINS)]
    for c in range(S5_CHAINS):
        scan(c)
        emit(c, us[c])


def _s5_core(h, w_in, layer, wb, wc, cst, tab, dvec):
    L, D = h.shape
    tm = S5_ROW_BLOCK
    nc = S5_CHAINS
    nb, cb, ns2 = wb.shape
    ns = ns2 // 2
    vmem = (2 * (tm * D * 2 + nc * (D * cb * 4 + 2 * cb * ns2 * 2 + 10 * 8 * ns * 4
                                    + S5_SEG * ns2 * 4 + tm * cb * 2) + tm * tm * 2)
            + nc * (tm * ns2 * 4 + D * cb * 2 + 4 * tm * cb * 4) + tm * ns2 * 4)
    return pl.pallas_call(
        functools.partial(_s5_kernel, tm=tm, ns=ns, cb=cb),
        grid=(nb // nc, L // tm),
        in_specs=[
            pl.BlockSpec((tm, D), lambda j, i: (i, 0)),
            pl.BlockSpec((None, D, nc * cb), lambda j, i: (layer, 0, j)),
            pl.BlockSpec((nc, cb, ns2), lambda j, i: (j, 0, 0)),
            pl.BlockSpec((nc, ns2, cb), lambda j, i: (j, 0, 0)),
            pl.BlockSpec((nc, 10, V7X_SUBLANES, ns), lambda j, i: (j, 0, 0, 0)),
            pl.BlockSpec((nc, S5_SEG, ns2), lambda j, i: (j, 0, 0)),
            pl.BlockSpec((nc, 1, cb), lambda j, i: (j, 0, 0)),
            pl.BlockSpec((tm, tm), lambda j, i: (0, 0)),
        ],
        out_specs=pl.BlockSpec((tm, nc * cb), lambda j, i: (i, j)),
        out_shape=jax.ShapeDtypeStruct((L, D), BF16),
        scratch_shapes=[pltpu.VMEM((nc, tm, ns2), F32),
                        pltpu.VMEM((nc, V7X_SUBLANES, ns), F32)],
        compiler_params=_cparams(("arbitrary", "arbitrary"), vmem),
        name="s5_core",
    )(h, w_in, wb, wc, cst, tab, dvec, _s5_unpermutation(tm))


def _glu_kernel(y_ref, wv_ref, wg_ref, x_ref, o_ref):
    y = y_ref[...]
    val = jnp.dot(y, wv_ref[...], preferred_element_type=F32)
    gate = jnp.dot(y, wg_ref[...], preferred_element_type=F32)
    o_ref[...] = x_ref[...] + val * (1.0 / (1.0 + jnp.exp(-gate)))


def _glu_residual(y, w_glu, layer, x, tm=2048, tn=256):
    L, D = y.shape
    nj = D // tn
    vmem = 2 * (tm * D * 2 + 2 * D * tn * 2 + 2 * tm * tn * 4) + 4 * tm * tn * 4
    return pl.pallas_call(
        _glu_kernel,
        grid=(L // tm, nj),
        in_specs=[pl.BlockSpec((tm, D), lambda i, j: (i, 0)),
                  pl.BlockSpec((None, D, tn), lambda i, j: (layer, 0, j)),
                  pl.BlockSpec((None, D, tn), lambda i, j: (layer, 0, nj + j)),
                  pl.BlockSpec((tm, tn), lambda i, j: (i, j))],
        out_specs=pl.BlockSpec((tm, tn), lambda i, j: (i, j)),
        out_shape=jax.ShapeDtypeStruct((L, D), F32),
        compiler_params=_cparams(("arbitrary", "arbitrary"), vmem),
        name="s5_glu",
    )(y, w_glu, w_glu, x)


def _ffn_kernel(x_ref, g_ref, wg_ref, wv_ref, cw_ref, cb_ref, wd_ref, *rest,
                tm, tf, final_norm):
    if final_norm:
        fg_ref, o_ref, h_ref, halo_ref, gs_ref = rest
    else:
        o_ref, h_ref, halo_ref, gs_ref = rest
    i = pl.program_id(0)
    f = pl.program_id(1)
    H = V7X_SUBLANES

    def row_chunks(fn):
        def body(c, carry):
            fn(pl.ds(pl.multiple_of(c * FFN_ROW_CHUNK, FFN_ROW_CHUNK), FFN_ROW_CHUNK))
            return carry
        lax.fori_loop(0, tm // FFN_ROW_CHUNK, body, 0)

    @pl.when(f == 0)
    def _():
        def normalize(rows):
            h_ref[rows, :] = _rms_normalize(x_ref[rows, :], g_ref[...]).astype(BF16)
            o_ref[rows, :] = jnp.zeros((FFN_ROW_CHUNK, o_ref.shape[1]), F32)
        row_chunks(normalize)

    h = h_ref[...]

    def up(cols):
        gate = jnp.dot(h, wg_ref[:, cols].astype(BF16), preferred_element_type=F32)
        val = jnp.dot(h, wv_ref[:, cols], preferred_element_type=F32)
        return gate, val

    def activate(cols, gate, val):
        prev = halo_ref[f, :, cols]
        gs_ref[0:H, cols] = jnp.where(i == 0, jnp.zeros_like(prev), prev)
        gs_ref[H:H + tm, cols] = gate
        halo_ref[f, :, cols] = gate[tm - H:, :]
        g1 = gs_ref[H - 1:H - 1 + tm, cols]
        g2 = gs_ref[H - 2:H - 2 + tm, cols]
        conv = (cw_ref[0:1, cols] * g2 + cw_ref[1:2, cols] * g1
                + cw_ref[2:3, cols] * gate + cb_ref[:, cols])
        return (_gelu_tanh(conv) * val).astype(BF16)

    groups = [slice(c, c + FFN_COL_GROUP) for c in range(0, tf, FFN_COL_GROUP)]
    ups = [up(cols) for cols in groups]
    for cols, (gate, val) in zip(groups, ups):
        act = activate(cols, gate, val)
        o_ref[...] += jnp.dot(act, wd_ref[cols, :].astype(BF16),
                              preferred_element_type=F32)

    @pl.when(f == pl.num_programs(1) - 1)
    def _():
        def finish(rows):
            y = x_ref[rows, :] + o_ref[rows, :]
            if final_norm:
                y = _rms_normalize(y, fg_ref[...])
            o_ref[rows, :] = y
        row_chunks(finish)


def _ffn(x, layer, ffn_norm, w_up, w_val, conv_w, conv_b, w_down, final_g=None,
         tm=1024, tf=512):
    L, D = x.shape
    n_layers, F, _ = w_down.shape
    nf = F // tf
    final_norm = final_g is not None
    in_specs = [
        pl.BlockSpec((tm, D), lambda i, f: (i, 0), pipeline_mode=pl.Buffered(1)),
        pl.BlockSpec((None, 1, D), lambda i, f: (layer, 0, 0)),
        pl.BlockSpec((None, D, tf), lambda i, f: (layer, 0, f)),
        pl.BlockSpec((None, D, tf), lambda i, f: (layer, 0, f)),
        pl.BlockSpec((None, CONV_WIDTH, tf), lambda i, f: (layer, 0, f)),
        pl.BlockSpec((None, 1, tf), lambda i, f: (layer, 0, f)),
        pl.BlockSpec((None, tf, D), lambda i, f: (layer, f, 0)),
    ]
    args = [x, ffn_norm.reshape(n_layers, 1, D), w_up, w_val, conv_w,
            conv_b.reshape(n_layers, 1, F), w_down]
    if final_norm:
        in_specs.append(pl.BlockSpec((1, D), lambda i, f: (0, 0)))
        args.append(final_g.reshape(1, D))
    vmem = (tm * D * 4 + 2 * tm * D * 4 + 2 * 5 * D * tf * 2 + 2 * D * tf * 2 + tm * D * 2
            + nf * V7X_SUBLANES * tf * 4 + (tm + 8) * tf * 4 + 6 * tm * tf * 4
            + tm * D * 4)
    return pl.pallas_call(
        functools.partial(_ffn_kernel, tm=tm, tf=tf, final_norm=final_norm),
        grid=(L // tm, nf),
        in_specs=in_specs,
        out_specs=pl.BlockSpec((tm, D), lambda i, f: (i, 0)),
        out_shape=jax.ShapeDtypeStruct((L, D), F32),
        scratch_shapes=[pltpu.VMEM((tm, D), BF16),
                        pltpu.VMEM((nf, V7X_SUBLANES, tf), F32),
                        pltpu.VMEM((tm + V7X_SUBLANES, tf), F32)],
        compiler_params=_cparams(("arbitrary", "arbitrary"), vmem),
        name="conv_glu_ffn",
    )(*args)


def _fgate_kernel(wft_ref, h_ref, bf_ref, o_ref):
    z = lax.dot_general(wft_ref[...], h_ref[...], (((1,), (1,)), ((), ())),
                        preferred_element_type=F32)
    z = z + bf_ref[...]
    o_ref[...] = jnp.minimum(z, 0.0) - jnp.log1p(jnp.exp(-jnp.abs(z)))


def _fgate_logf(h_kv, w_f, b_f, tm=1024):
    L, D = h_kv.shape
    H = w_f.shape[1]
    hp = V7X_LANES
    wft = jnp.zeros((hp, D), BF16).at[:H].set(w_f.T.astype(BF16))
    bfp = jnp.zeros((hp, 1), F32).at[:H, 0].set(b_f.astype(F32))
    return pl.pallas_call(
        _fgate_kernel,
        grid=(L // tm,),
        in_specs=[pl.BlockSpec((hp, D), lambda i: (0, 0)),
                  pl.BlockSpec((tm, D), lambda i: (i, 0)),
                  pl.BlockSpec((hp, 1), lambda i: (0, 0))],
        out_specs=pl.BlockSpec((hp, tm), lambda i: (0, i)),
        out_shape=jax.ShapeDtypeStruct((hp, L), F32),
        compiler_params=_cparams(("arbitrary",), 2 * (tm * D * 2 + hp * D * 2)
                                 + 8 * hp * tm * 4),
        name="fgate_logf",
    )(wft, h_kv, bfp)


def _cumsum_kernel(x_ref, *o_refs, tb):
    r = lax.broadcasted_iota(jnp.int32, (tb, tb), 0)
    c = lax.broadcasted_iota(jnp.int32, (tb, tb), 1)
    tri = (r <= c).astype(F32)

    def body(b, carry):
        off = pl.multiple_of(b * tb, tb)
        cs = jnp.dot(x_ref[:, pl.ds(off, tb)], tri,
                     precision=lax.Precision.HIGHEST,
                     preferred_element_type=F32) + carry
        rem = cs * (-LOG2E)
        for o_ref in o_refs:
            piece = rem.astype(BF16)
            o_ref[:, pl.ds(off, tb)] = piece
            rem = rem - piece.astype(F32)
        return cs[:, tb - 1:tb]

    lax.fori_loop(0, x_ref.shape[1] // tb, body,
                  jnp.zeros((x_ref.shape[0], 1), F32))


def _cumsum_pieces(x, tb=256):
    R, L = x.shape
    return pl.pallas_call(
        functools.partial(_cumsum_kernel, tb=tb),
        out_shape=[jax.ShapeDtypeStruct((R, L), BF16)] * CK_PIECES,
        compiler_params=pltpu.CompilerParams(
            vmem_limit_bytes=min(8 * R * L * 4, V7X_VMEM_LIMIT_BYTES)),
        name="logf_cumsum",
    )(x)


def _attn_kernel(q_ref, k_ref, c_ref, v_ref, o_ref, acc_ref, m_ref, l_ref, s_ref,
                 *, bq, bk, dh, n_heads):
    head = pl.program_id(0)
    qi = pl.program_id(1)
    W = V7X_LANES
    n_lane_tiles = bk // W
    lane = lax.broadcasted_iota(jnp.int32, (bq, dh), 1)
    pick = (lane % n_heads == head) & (lane < CK_PIECES * n_heads)
    qa = jnp.concatenate([q_ref[...], pick.astype(BF16)], axis=1)
    acc_ref[...] = jnp.zeros_like(acc_ref)
    m_ref[...] = jnp.full_like(m_ref, MASK_VALUE)
    l_ref[...] = jnp.zeros_like(l_ref)

    def scores(kj):
        off = pl.multiple_of(kj * bk, bk)
        ka = jnp.concatenate([k_ref[pl.ds(off, bk), :], c_ref[pl.ds(off, bk), :]],
                             axis=1)
        return lax.dot_general(qa, ka, (((1,), (1,)), ((), ())),
                               preferred_element_type=F32)

    s_ref[0] = scores(0)

    def step(kj, slot, masked, last):
        off = pl.multiple_of(kj * bk, bk)
        if not last:
            s_ref[1 - slot] = scores(kj + 1)
        tiles = []
        for t in range(n_lane_tiles):
            s = s_ref[slot, :, t * W:(t + 1) * W]
            if masked:
                key = off + t * W + lax.broadcasted_iota(jnp.int32, (bq, W), 1)
                qry = qi * bq + lax.broadcasted_iota(jnp.int32, (bq, W), 0)
                s = jnp.where(key <= qry, s, MASK_VALUE)
            tiles.append(s)
        tile_max = functools.reduce(jnp.maximum, tiles)
        row_max = jnp.broadcast_to(jnp.max(tile_max, axis=1, keepdims=True), (bq, W))
        m_old = m_ref[...]
        m_new = jnp.maximum(m_old, row_max)
        alpha = jnp.exp2(m_old - m_new)
        m_ref[...] = m_new
        p = [jnp.exp2(s - m_new) for s in tiles]
        l_ref[...] = alpha * l_ref[...] + functools.reduce(lambda a, b: a + b, p)
        pv = jnp.dot(jnp.concatenate([x.astype(BF16) for x in p], axis=1),
                     v_ref[pl.ds(off, bk), :], preferred_element_type=F32)
        acc_ref[...] = alpha * acc_ref[...] + pv

    n_diag = bq // bk
    unroll = max(2, ATTN_KEYS_PER_ITER // bk)
    assert dh == W and unroll % 2 == 0 and unroll % n_diag == 0

    def full_steps(kj0, n):
        for d in range(n):
            step(kj0 + d, d % 2, False, False)

    def body(jj, carry):
        full_steps(unroll * jj, unroll)
        return carry

    n_full = qi * n_diag
    lax.fori_loop(0, n_full // unroll, body, 0)
    rem = n_full % unroll
    for r in range(0, unroll, n_diag):
        @pl.when(rem == r)
        def _(r=r):
            full_steps(n_full - r, r)
            for dj in range(n_diag):
                step(n_full + dj, (r + dj) % 2, True, dj == n_diag - 1)

    l = jnp.sum(l_ref[...], axis=1, keepdims=True)
    o_ref[...] = (acc_ref[...] / l).astype(o_ref.dtype)


def _fox_attention(q, k, v, caug, n_heads, bq=1024, bk=1024):
    L, D = q.shape
    dh = D // n_heads
    assert CK_PIECES * n_heads <= dh
    vmem = (2 * (3 * L * dh * 2 + 2 * bq * dh * 2) + 3 * bq * dh * 4 + 2 * bq * bk * 4
            + 4 * bq * bk * 4)
    return pl.pallas_call(
        functools.partial(_attn_kernel, bq=bq, bk=bk, dh=dh, n_heads=n_heads),
        grid=(n_heads, L // bq),
        in_specs=[pl.BlockSpec((bq, dh), lambda h, i: (i, h)),
                  pl.BlockSpec((L, dh), lambda h, i: (0, h)),
                  pl.BlockSpec((L, dh), lambda h, i: (0, 0)),
                  pl.BlockSpec((L, dh), lambda h, i: (0, h))],
        out_specs=pl.BlockSpec((bq, dh), lambda h, i: (i, h)),
        out_shape=jax.ShapeDtypeStruct((L, D), BF16),
        scratch_shapes=[pltpu.VMEM((bq, dh), F32), pltpu.VMEM((bq, V7X_LANES), F32),
                        pltpu.VMEM((bq, V7X_LANES), F32),
                        pltpu.VMEM((2, bq, bk), F32)],
        compiler_params=_cparams(("arbitrary", "arbitrary"), vmem),
        name="fox_attention",
    )(q, k, caug, v)


def kernel(x, a_norm, a_w_in, a_lambda_re, a_lambda_im, a_log_step, a_b_re, a_b_im, a_c_re, a_c_im, a_d, a_w_glu, kv_norm, w_k, w_v, w_f, b_f, b_norm, b_w_q, b_w_o, ffn_norm, ffn_w_up, ffn_conv_w, ffn_conv_b, ffn_w_down, final_norm):
    bsz, seq, d_model = x.shape
    n_a = a_w_in.shape[0]
    n_b = b_w_q.shape[0]
    n_heads = w_f.shape[1]
    dh = d_model // n_heads
    assert bsz == 1 and n_b >= 1
    xs = x.reshape(seq, d_model).astype(F32)
    d_ff = ffn_w_down.shape[1]
    w_val = _cast_bf16(ffn_w_up, col_range=(d_ff, 2 * d_ff))
    w_down = ffn_w_down
    w_glu = _cast_bf16(a_w_glu)

    def ffn(xs, li, final_g=None):
        return _ffn(xs, li, ffn_norm, ffn_w_up, w_val, ffn_conv_w, ffn_conv_b, w_down,
                    final_g=final_g)

    for i in range(n_a):
        h = _rmsnorm(xs, a_norm[i], BF16, tm=S5_ROW_BLOCK, segment_rows=S5_SEG)
        prep = _s5_prepare(a_lambda_re[i], a_lambda_im[i], a_log_step[i],
                           a_b_re[i], a_b_im[i], a_c_re[i], a_c_im[i], a_d[i])
        y = _s5_core(h, a_w_in, i, *prep)
        xs = _glu_residual(y, w_glu, i, xs)
        xs = ffn(xs, i)

    h_kv = _rmsnorm(xs, kv_norm, BF16)
    k = _matmul(h_kv, w_k, 0, BF16, name="k_proj")
    v = _matmul(h_kv, w_v, 0, BF16, name="v_proj")
    pieces = _cumsum_pieces(_fgate_logf(h_kv, w_f, b_f))
    caug = jnp.concatenate([p[:n_heads] for p in pieces], axis=0).T
    caug = jnp.pad(caug, ((0, 0), (0, dh - CK_PIECES * n_heads)))

    q_scale = dh ** -0.5 * LOG2E
    for j in range(n_b):
        li = n_a + j
        h = _rmsnorm(xs, b_norm[j], BF16)
        q = _matmul(h, b_w_q, j, BF16, out_scale=q_scale, name="q_proj")
        o = _fox_attention(q, k, v, caug, n_heads)
        xs = _matmul(o, b_w_o, j, F32, residual=xs, name="o_proj")
        xs = ffn(xs, li, final_g=final_norm if j == n_b - 1 else None)

    return xs.reshape(bsz, seq, d_model).astype(x.dtype)
```

```python
import functools
import math

import jax
import jax.numpy as jnp
from jax import lax
from jax.experimental import pallas as pl
from jax.experimental.pallas import tpu as pltpu

F32 = jnp.float32
BF16 = jnp.bfloat16

NORM_EPS = 1e-6
CONV_WIDTH = 3
LOG2E = math.log2(math.e)

V7X_SUBLANES = 8
V7X_LANES = 128
V7X_VMEM_LIMIT_BYTES = 58 * 1024 * 1024

S5_GROUPS_PER_BLOCK = 16
S5_ROW_BLOCK = 512
S5_SEG = S5_ROW_BLOCK // V7X_SUBLANES
S5_CHAINS = 2
FFN_ROW_CHUNK = 128
FFN_COL_GROUP = 256
CK_PIECES = 3
ATTN_KEYS_PER_ITER = 2048
MASK_VALUE = -1e30


def _cparams(semantics, vmem_bytes):
    budget = int(vmem_bytes) * 5 // 4 + (4 << 20)
    return pltpu.CompilerParams(
        dimension_semantics=semantics,
        vmem_limit_bytes=min(budget, V7X_VMEM_LIMIT_BYTES))


def _gelu_tanh(x):
    c = math.sqrt(2.0 / math.pi)
    return x * (0.5 * (1.0 + jnp.tanh(c * (x + 0.044715 * (x * x * x)))))


def _rms_normalize(x, g):
    ms = jnp.mean(x * x, axis=-1, keepdims=True)
    return x * lax.rsqrt(ms + NORM_EPS) * g


def _cmul(ar, ai, br, bi):
    return ar * br - ai * bi, ar * bi + ai * br


def _stacked(w):
    return w if w.ndim == 3 else w[None]


def _cast_kernel(w_ref, o_ref):
    o_ref[...] = w_ref[...].astype(o_ref.dtype)


def _cast_bf16(w, col_range=None, tr=512, max_cols=4096):
    shape = w.shape
    c0, c1 = col_range if col_range is not None else (0, shape[-1])
    C = c1 - c0
    R = math.prod(shape[:-1])
    tc = C
    while tc > max_cols:
        tc //= 2
    assert C % tc == 0 and c0 % tc == 0 and tc % V7X_LANES == 0 and R % tr == 0
    out = pl.pallas_call(
        _cast_kernel,
        grid=(R // tr, C // tc),
        in_specs=[pl.BlockSpec((tr, tc), lambda i, j: (i, j + c0 // tc))],
        out_specs=pl.BlockSpec((tr, tc), lambda i, j: (i, j)),
        out_shape=jax.ShapeDtypeStruct((R, C), BF16),
        compiler_params=_cparams(("arbitrary", "arbitrary"), 2 * tr * tc * 6),
        name="cast_bf16",
    )(w.reshape(R, shape[-1]))
    return out.reshape(shape[:-1] + (C,))


def _rmsnorm_kernel(x_ref, g_ref, *rest):
    o_ref = rest[-1]
    h = _rms_normalize(x_ref[...], g_ref[...]).astype(o_ref.dtype)
    if len(rest) == 2:
        h = jnp.dot(rest[0][...], h, preferred_element_type=F32).astype(o_ref.dtype)
    o_ref[...] = h


def _segment_interleave(tm, segment_rows):
    n_seg = tm // segment_rows
    r = jnp.arange(tm)
    src = (r % n_seg) * segment_rows + r // n_seg
    return (src[:, None] == jnp.arange(tm)[None, :]).astype(BF16)


def _rmsnorm(x, g, out_dtype, tm=512, segment_rows=None):
    L, D = x.shape
    in_specs = [pl.BlockSpec((tm, D), lambda i: (i, 0)),
                pl.BlockSpec((1, D), lambda i: (0, 0))]
    args = [x, g.reshape(1, D)]
    if segment_rows is not None:
        assert out_dtype == BF16
        in_specs.append(pl.BlockSpec((tm, tm), lambda i: (0, 0)))
        args.append(_segment_interleave(tm, segment_rows))
    return pl.pallas_call(
        _rmsnorm_kernel,
        grid=(L // tm,),
        in_specs=in_specs,
        out_specs=pl.BlockSpec((tm, D), lambda i: (i, 0)),
        out_shape=jax.ShapeDtypeStruct((L, D), out_dtype),
        compiler_params=_cparams(("arbitrary",), 4 * tm * D * 8),
        name="rmsnorm",
    )(*args)


def _matmul_kernel(a_ref, b_ref, o_ref, *, out_scale):
    acc = jnp.dot(a_ref[...], b_ref[...].astype(BF16), preferred_element_type=F32)
    if out_scale is not None:
        acc = acc * out_scale
    o_ref[...] = acc.astype(o_ref.dtype)


def _matmul_res_kernel(a_ref, b_ref, r_ref, o_ref):
    o_ref[...] = r_ref[...] + jnp.dot(a_ref[...], b_ref[...].astype(BF16),
                                      preferred_element_type=F32)


def _matmul(a, w, layer, out_dtype, residual=None, out_scale=None, tm=2048, tn=512,
            name="matmul"):
    L, K = a.shape
    w = _stacked(w)
    N = w.shape[2]
    in_specs = [pl.BlockSpec((tm, K), lambda i, j: (i, 0)),
                pl.BlockSpec((None, K, tn), lambda i, j: (layer, 0, j))]
    args = [a, w]
    if residual is not None:
        in_specs.append(pl.BlockSpec((tm, tn), lambda i, j: (i, j)))
        args.append(residual)
        kern = _matmul_res_kernel
    else:
        kern = functools.partial(_matmul_kernel, out_scale=out_scale)
    vmem = 2 * (tm * K * 2 + K * tn * 4 + 2 * tm * tn * 4) + K * tn * 2 + 2 * tm * tn * 4
    return pl.pallas_call(
        kern,
        grid=(L // tm, N // tn),
        in_specs=in_specs,
        out_specs=pl.BlockSpec((tm, tn), lambda i, j: (i, j)),
        out_shape=jax.ShapeDtypeStruct((L, N), out_dtype),
        compiler_params=_cparams(("arbitrary", "arbitrary"), vmem),
        name=name,
    )(*args)


def _cpowers(br, bi, n):
    pr, pi = br[None], bi[None]
    k = 1
    while k < n:
        tr, ti = _cmul(pr, pi, pr[k - 1:k], pi[k - 1:k])
        pr = jnp.concatenate([pr, tr], axis=0)
        pi = jnp.concatenate([pi, ti], axis=0)
        k *= 2
    return pr, pi


def _s5_prepare(lam_re, lam_im, log_step, b_re, b_im, c_re, c_im, d):
    G, P = lam_re.shape
    C = b_re.shape[-1]
    gb = S5_GROUPS_PER_BLOCK
    nb = G // gb
    S = V7X_SUBLANES
    lam_re = lam_re.astype(F32)
    lam_im = lam_im.astype(F32)
    dt = jnp.exp(log_step.astype(F32))[:, None]
    mag = jnp.exp(lam_re * dt)
    lbr = mag * jnp.cos(lam_im * dt)
    lbi = mag * jnp.sin(lam_im * dt)
    den = lam_re * lam_re + lam_im * lam_im
    fr = ((lbr - 1.0) * lam_re + lbi * lam_im) / den
    fi = (lbi * lam_re - (lbr - 1.0) * lam_im) / den
    bbr, bbi = _cmul(fr[..., None], fi[..., None], b_re.astype(F32), b_im.astype(F32))
    def block_diag(m):
        a, b = m.shape[1:]
        mt = m.reshape(nb, gb, a, b).transpose(0, 1, 3, 2).reshape(nb, gb * b, a)
        same = ((jnp.arange(gb * b) // b)[:, None]
                == (jnp.arange(gb * a) // a)[None, :])
        return jnp.where(same, jnp.tile(mt, (1, 1, gb)), 0.0)

    wb = jnp.concatenate([block_diag(bbr), block_diag(bbi)], axis=-1).astype(BF16)
    wc = jnp.concatenate([block_diag(c_re.astype(F32)), block_diag(-c_im.astype(F32))],
                         axis=-2).astype(BF16)

    def lanes(m):
        return m.reshape(m.shape[0], nb, gb * P).transpose(1, 0, 2)

    tr, ti = _cpowers(lbr, lbi, S5_SEG)
    tab = jnp.concatenate([lanes(tr), lanes(ti)], axis=-1)
    sr, si = _cpowers(tr[-1], ti[-1], S)
    rows = jnp.arange(S)[:, None, None]
    kinds = [jnp.broadcast_to(lbr[None], (S, G, P)),
             jnp.broadcast_to(lbi[None], (S, G, P))]
    for k in (1, 2, 4):
        kinds += [jnp.where(rows >= k, sr[k - 1][None], 0.0),
                  jnp.where(rows >= k, si[k - 1][None], 0.0)]
    kinds += [sr, si]
    cst = jnp.stack([lanes(m) for m in kinds], axis=1)
    dvec = d.astype(F32).reshape(nb, 1, gb * C)
    return wb, wc, cst, tab, dvec


def _s5_unpermutation(tm):
    l = jnp.arange(tm)
    src = (l % S5_SEG) * V7X_SUBLANES + l // S5_SEG
    return (src[:, None] == jnp.arange(tm)[None, :]).astype(BF16)


def _s5_kernel(h_ref, win_ref, wb_ref, wc_ref, cst_ref, tab_ref, d_ref,
               unperm_ref, o_ref, st_ref, carry_ref, *, tm, ns, cb):
    S = V7X_SUBLANES
    n_tiles = tm // S

    @pl.when(pl.program_id(1) == 0)
    def _():
        carry_ref[...] = jnp.zeros_like(carry_ref)

    h = h_ref[...]

    def project(c):
        u = jnp.dot(h, win_ref[:, c * cb:(c + 1) * cb].astype(BF16),
                    preferred_element_type=F32)
        st_ref[c] = jnp.dot(u.astype(BF16), wb_ref[c],
                            preferred_element_type=F32)
        return u

    def scan(c):
        ar = cst_ref[c, 0]
        ai = cst_ref[c, 1]
        dr = jnp.zeros((S, ns), F32)
        di = jnp.zeros((S, ns), F32)
        for t in range(n_tiles):
            rows = slice(t * S, (t + 1) * S)
            dr, di = (ar * dr - ai * di + st_ref[c, rows, 0:ns],
                      ar * di + ai * dr + st_ref[c, rows, ns:2 * ns])
            st_ref[c, rows, 0:ns] = dr
            st_ref[c, rows, ns:2 * ns] = di
        for lvl, k in enumerate((1, 2, 4)):
            br = cst_ref[c, 2 + 2 * lvl]
            bi = cst_ref[c, 3 + 2 * lvl]
            sr = pltpu.roll(dr, k, 0)
            si = pltpu.roll(di, k, 0)
            dr, di = dr + br * sr - bi * si, di + br * si + bi * sr
        cin_r = carry_ref[c, 0:1, :]
        cin_i = carry_ref[c, 1:2, :]
        pr = cst_ref[c, 8]
        pi = cst_ref[c, 9]
        dr, di = dr + pr * cin_r - pi * cin_i, di + pr * cin_i + pi * cin_r
        first = lax.broadcasted_iota(jnp.int32, (S, ns), 0) == 0
        cr = jnp.where(first, cin_r, pltpu.roll(dr, 1, 0))
        ci = jnp.where(first, cin_i, pltpu.roll(di, 1, 0))
        carry_ref[c, 0:1, :] = dr[S - 1:S, :]
        carry_ref[c, 1:2, :] = di[S - 1:S, :]
        for t in range(n_tiles):
            rows = slice(t * S, (t + 1) * S)
            tr = tab_ref[c, t:t + 1, 0:ns]
            ti = tab_ref[c, t:t + 1, ns:2 * ns]
            st_ref[c, rows, 0:ns] += tr * cr - ti * ci
            st_ref[c, rows, ns:2 * ns] += tr * ci + ti * cr

    def emit(c, u):
        y = jnp.dot(st_ref[c].astype(BF16), wc_ref[c],
                    preferred_element_type=F32) + d_ref[c] * u
        act = _gelu_tanh(y).astype(BF16)
        o_ref[:, c * cb:(c + 1) * cb] = jnp.dot(
            unperm_ref[...], act, preferred_element_type=F32).astype(o_ref.dtype)

    us = [project(c) for c in range(S5_CHAINS)]
    for c in range(S5_CHAINS):
        scan(c)
        emit(c, us[c])


def _s5_core(h, w_in, layer, wb, wc, cst, tab, dvec):
    L, D = h.shape
    tm = S5_ROW_BLOCK
    nc = S5_CHAINS
    nb, cb, ns2 = wb.shape
    ns = ns2 // 2
    vmem = (2 * (tm * D * 2 + nc * (D * cb * 4 + 2 * cb * ns2 * 2 + 10 * 8 * ns * 4
                                    + S5_SEG * ns2 * 4 + tm * cb * 2) + tm * tm * 2)
            + nc * (tm * ns2 * 4 + D * cb * 2 + 4 * tm * cb * 4) + tm * ns2 * 4)
    return pl.pallas_call(
        functools.partial(_s5_kernel, tm=tm, ns=ns, cb=cb),
        grid=(nb // nc, L // tm),
        in_specs=[
            pl.BlockSpec((tm, D), lambda j, i: (i, 0)),
            pl.BlockSpec((None, D, nc * cb), lambda j, i: (layer, 0, j)),
            pl.BlockSpec((nc, cb, ns2), lambda j, i: (j, 0, 0)),
            pl.BlockSpec((nc, ns2, cb), lambda j, i: (j, 0, 0)),
            pl.BlockSpec((nc, 10, V7X_SUBLANES, ns), lambda j, i: (j, 0, 0, 0)),
            pl.BlockSpec((nc, S5_SEG, ns2), lambda j, i: (j, 0, 0)),
            pl.BlockSpec((nc, 1, cb), lambda j, i: (j, 0, 0)),
            pl.BlockSpec((tm, tm), lambda j, i: (0, 0)),
        ],
        out_specs=pl.BlockSpec((tm, nc * cb), lambda j, i: (i, j)),
        out_shape=jax.ShapeDtypeStruct((L, D), BF16),
        scratch_shapes=[pltpu.VMEM((nc, tm, ns2), F32),
                        pltpu.VMEM((nc, V7X_SUBLANES, ns), F32)],
        compiler_params=_cparams(("arbitrary", "arbitrary"), vmem),
        name="s5_core",
    )(h, w_in, wb, wc, cst, tab, dvec, _s5_unpermutation(tm))


def _glu_kernel(y_ref, wv_ref, wg_ref, x_ref, o_ref):
    y = y_ref[...]
    val = jnp.dot(y, wv_ref[...], preferred_element_type=F32)
    gate = jnp.dot(y, wg_ref[...], preferred_element_type=F32)
    o_ref[...] = x_ref[...] + val * (1.0 / (1.0 + jnp.exp(-gate)))


def _glu_residual(y, w_glu, layer, x, tm=2048, tn=256):
    L, D = y.shape
    nj = D // tn
    vmem = 2 * (tm * D * 2 + 2 * D * tn * 2 + 2 * tm * tn * 4) + 4 * tm * tn * 4
    return pl.pallas_call(
        _glu_kernel,
        grid=(L // tm, nj),
        in_specs=[pl.BlockSpec((tm, D), lambda i, j: (i, 0)),
                  pl.BlockSpec((None, D, tn), lambda i, j: (layer, 0, j)),
                  pl.BlockSpec((None, D, tn), lambda i, j: (layer, 0, nj + j)),
                  pl.BlockSpec((tm, tn), lambda i, j: (i, j))],
        out_specs=pl.BlockSpec((tm, tn), lambda i, j: (i, j)),
        out_shape=jax.ShapeDtypeStruct((L, D), F32),
        compiler_params=_cparams(("arbitrary", "arbitrary"), vmem),
        name="s5_glu",
    )(y, w_glu, w_glu, x)


def _ffn_kernel(x_ref, g_ref, wg_ref, wv_ref, cw_ref, cb_ref, wd_ref, *rest,
                tm, tf, final_norm):
    if final_norm:
        fg_ref, o_ref, h_ref, halo_ref, gs_ref = rest
    else:
        o_ref, h_ref, halo_ref, gs_ref = rest
    i = pl.program_id(0)
    f = pl.program_id(1)
    H = V7X_SUBLANES

    def row_chunks(fn):
        def body(c, carry):
            fn(pl.ds(pl.multiple_of(c * FFN_ROW_CHUNK, FFN_ROW_CHUNK), FFN_ROW_CHUNK))
            return carry
        lax.fori_loop(0, tm // FFN_ROW_CHUNK, body, 0)

    @pl.when(f == 0)
    def _():
        def normalize(rows):
            h_ref[rows, :] = _rms_normalize(x_ref[rows, :], g_ref[...]).astype(BF16)
            o_ref[rows, :] = jnp.zeros((FFN_ROW_CHUNK, o_ref.shape[1]), F32)
        row_chunks(normalize)

    h = h_ref[...]

    def up(cols):
        gate = jnp.dot(h, wg_ref[:, cols].astype(BF16), preferred_element_type=F32)
        val = jnp.dot(h, wv_ref[:, cols], preferred_element_type=F32)
        return gate, val

    def activate(cols, gate, val):
        prev = halo_ref[f, :, cols]
        gs_ref[0:H, cols] = jnp.where(i == 0, jnp.zeros_like(prev), prev)
        gs_ref[H:H + tm, cols] = gate
        halo_ref[f, :, cols] = gate[tm - H:, :]
        g1 = gs_ref[H - 1:H - 1 + tm, cols]
        g2 = gs_ref[H - 2:H - 2 + tm, cols]
        conv = (cw_ref[0:1, cols] * g2 + cw_ref[1:2, cols] * g1
                + cw_ref[2:3, cols] * gate + cb_ref[:, cols])
        return (_gelu_tanh(conv) * val).astype(BF16)

    groups = [slice(c, c + FFN_COL_GROUP) for c in range(0, tf, FFN_COL_GROUP)]
    ups = [up(cols) for cols in groups]
    for cols, (gate, val) in zip(groups, ups):
        act = activate(cols, gate, val)
        o_ref[...] += jnp.dot(act, wd_ref[cols, :].astype(BF16),
                              preferred_element_type=F32)

    @pl.when(f == pl.num_programs(1) - 1)
    def _():
        def finish(rows):
            y = x_ref[rows, :] + o_ref[rows, :]
            if final_norm:
                y = _rms_normalize(y, fg_ref[...])
            o_ref[rows, :] = y
        row_chunks(finish)


def _ffn(x, layer, ffn_norm, w_up, w_val, conv_w, conv_b, w_down, final_g=None,
         tm=1024, tf=512):
    L, D = x.shape
    n_layers, F, _ = w_down.shape
    nf = F // tf
    final_norm = final_g is not None
    in_specs = [
        pl.BlockSpec((tm, D), lambda i, f: (i, 0), pipeline_mode=pl.Buffered(1)),
        pl.BlockSpec((None, 1, D), lambda i, f: (layer, 0, 0)),
        pl.BlockSpec((None, D, tf), lambda i, f: (layer, 0, f)),
        pl.BlockSpec((None, D, tf), lambda i, f: (layer, 0, f)),
        pl.BlockSpec((None, CONV_WIDTH, tf), lambda i, f: (layer, 0, f)),
        pl.BlockSpec((None, 1, tf), lambda i, f: (layer, 0, f)),
        pl.BlockSpec((None, tf, D), lambda i, f: (layer, f, 0)),
    ]
    args = [x, ffn_norm.reshape(n_layers, 1, D), w_up, w_val, conv_w,
            conv_b.reshape(n_layers, 1, F), w_down]
    if final_norm:
        in_specs.append(pl.BlockSpec((1, D), lambda i, f: (0, 0)))
        args.append(final_g.reshape(1, D))
    vmem = (tm * D * 4 + 2 * tm * D * 4 + 2 * 5 * D * tf * 2 + 2 * D * tf * 2 + tm * D * 2
            + nf * V7X_SUBLANES * tf * 4 + (tm + 8) * tf * 4 + 6 * tm * tf * 4
            + tm * D * 4)
    return pl.pallas_call(
        functools.partial(_ffn_kernel, tm=tm, tf=tf, final_norm=final_norm),
        grid=(L // tm, nf),
        in_specs=in_specs,
        out_specs=pl.BlockSpec((tm, D), lambda i, f: (i, 0)),
        out_shape=jax.ShapeDtypeStruct((L, D), F32),
        scratch_shapes=[pltpu.VMEM((tm, D), BF16),
                        pltpu.VMEM((nf, V7X_SUBLANES, tf), F32),
                        pltpu.VMEM((tm + V7X_SUBLANES, tf), F32)],
        compiler_params=_cparams(("arbitrary", "arbitrary"), vmem),
        name="conv_glu_ffn",
    )(*args)


def _fgate_kernel(wft_ref, h_ref, bf_ref, o_ref):
    z = lax.dot_general(wft_ref[...], h_ref[...], (((1,), (1,)), ((), ())),
                        preferred_element_type=F32)
    z = z + bf_ref[...]
    o_ref[...] = jnp.minimum(z, 0.0) - jnp.log1p(jnp.exp(-jnp.abs(z)))


def _fgate_logf(h_kv, w_f, b_f, tm=1024):
    L, D = h_kv.shape
    H = w_f.shape[1]
    hp = V7X_LANES
    wft = jnp.zeros((hp, D), BF16).at[:H].set(w_f.T.astype(BF16))
    bfp = jnp.zeros((hp, 1), F32).at[:H, 0].set(b_f.astype(F32))
    return pl.pallas_call(
        _fgate_kernel,
        grid=(L // tm,),
        in_specs=[pl.BlockSpec((hp, D), lambda i: (0, 0)),
                  pl.BlockSpec((tm, D), lambda i: (i, 0)),
                  pl.BlockSpec((hp, 1), lambda i: (0, 0))],
        out_specs=pl.BlockSpec((hp, tm), lambda i: (0, i)),
        out_shape=jax.ShapeDtypeStruct((hp, L), F32),
        compiler_params=_cparams(("arbitrary",), 2 * (tm * D * 2 + hp * D * 2)
                                 + 8 * hp * tm * 4),
        name="fgate_logf",
    )(wft, h_kv, bfp)


def _cumsum_kernel(x_ref, *o_refs, tb):
    r = lax.broadcasted_iota(jnp.int32, (tb, tb), 0)
    c = lax.broadcasted_iota(jnp.int32, (tb, tb), 1)
    tri = (r <= c).astype(F32)

    def body(b, carry):
        off = pl.multiple_of(b * tb, tb)
        cs = jnp.dot(x_ref[:, pl.ds(off, tb)], tri,
                     precision=lax.Precision.HIGHEST,
                     preferred_element_type=F32) + carry
        rem = cs * (-LOG2E)
        for o_ref in o_refs:
            piece = rem.astype(BF16)
            o_ref[:, pl.ds(off, tb)] = piece
            rem = rem - piece.astype(F32)
        return cs[:, tb - 1:tb]

    lax.fori_loop(0, x_ref.shape[1] // tb, body,
                  jnp.zeros((x_ref.shape[0], 1), F32))


def _cumsum_pieces(x, tb=256):
    R, L = x.shape
    return pl.pallas_call(
        functools.partial(_cumsum_kernel, tb=tb),
        out_shape=[jax.ShapeDtypeStruct((R, L), BF16)] * CK_PIECES,
        compiler_params=pltpu.CompilerParams(
            vmem_limit_bytes=min(8 * R * L * 4, V7X_VMEM_LIMIT_BYTES)),
        name="logf_cumsum",
    )(x)


def _attn_kernel(q_ref, k_ref, c_ref, v_ref, o_ref, acc_ref, m_ref, l_ref, s_ref,
                 *, bq, bk, dh, n_heads):
    head = pl.program_id(0)
    qi = pl.program_id(1)
    W = V7X_LANES
    n_lane_tiles = bk // W
    lane = lax.broadcasted_iota(jnp.int32, (bq, dh), 1)
    pick = (lane % n_heads == head) & (lane < CK_PIECES * n_heads)
    qa = jnp.concatenate([q_ref[...], pick.astype(BF16)], axis=1)
    acc_ref[...] = jnp.zeros_like(acc_ref)
    m_ref[...] = jnp.full_like(m_ref, MASK_VALUE)
    l_ref[...] = jnp.zeros_like(l_ref)

    def scores(kj):
        off = pl.multiple_of(kj * bk, bk)
        ka = jnp.concatenate([k_ref[pl.ds(off, bk), :], c_ref[pl.ds(off, bk), :]],
                             axis=1)
        return lax.dot_general(qa, ka, (((1,), (1,)), ((), ())),
                               preferred_element_type=F32)

    s_ref[0] = scores(0)

    def step(kj, slot, masked, last):
        off = pl.multiple_of(kj * bk, bk)
        if not last:
            s_ref[1 - slot] = scores(kj + 1)
        tiles = []
        for t in range(n_lane_tiles):
            s = s_ref[slot, :, t * W:(t + 1) * W]
            if masked:
                key = off + t * W + lax.broadcasted_iota(jnp.int32, (bq, W), 1)
                qry = qi * bq + lax.broadcasted_iota(jnp.int32, (bq, W), 0)
                s = jnp.where(key <= qry, s, MASK_VALUE)
            tiles.append(s)
        tile_max = functools.reduce(jnp.maximum, tiles)
        row_max = jnp.broadcast_to(jnp.max(tile_max, axis=1, keepdims=True), (bq, W))
        m_old = m_ref[...]
        m_new = jnp.maximum(m_old, row_max)
        alpha = jnp.exp2(m_old - m_new)
        m_ref[...] = m_new
        p = [jnp.exp2(s - m_new) for s in tiles]
        l_ref[...] = alpha * l_ref[...] + functools.reduce(lambda a, b: a + b, p)
        pv = jnp.dot(jnp.concatenate([x.astype(BF16) for x in p], axis=1),
                     v_ref[pl.ds(off, bk), :], preferred_element_type=F32)
        acc_ref[...] = alpha * acc_ref[...] + pv

    n_diag = bq // bk
    unroll = max(2, ATTN_KEYS_PER_ITER // bk)
    assert dh == W and unroll % 2 == 0 and unroll % n_diag == 0

    def full_steps(kj0, n):
        for d in range(n):
            step(kj0 + d, d % 2, False, False)

    def body(jj, carry):
        full_steps(unroll * jj, unroll)
        return carry

    n_full = qi * n_diag
    lax.fori_loop(0, n_full // unroll, body, 0)
    rem = n_full % unroll
    for r in range(0, unroll, n_diag):
        @pl.when(rem == r)
        def _(r=r):
            full_steps(n_full - r, r)
            for dj in range(n_diag):
                step(n_full + dj, (r + dj) % 2, True, dj == n_diag - 1)

    l = jnp.sum(l_ref[...], axis=1, keepdims=True)
    o_ref[...] = (acc_ref[...] / l).astype(o_ref.dtype)


def _fox_attention(q, k, v, caug, n_heads, bq=1024, bk=1024):
    L, D = q.shape
    dh = D // n_heads
    assert CK_PIECES * n_heads <= dh
    vmem = (2 * (3 * L * dh * 2 + 2 * bq * dh * 2) + 3 * bq * dh * 4 + 2 * bq * bk * 4
            + 4 * bq * bk * 4)
    return pl.pallas_call(
        functools.partial(_attn_kernel, bq=bq, bk=bk, dh=dh, n_heads=n_heads),
        grid=(n_heads, L // bq),
        in_specs=[pl.BlockSpec((bq, dh), lambda h, i: (i, h)),
                  pl.BlockSpec((L, dh), lambda h, i: (0, h)),
                  pl.BlockSpec((L, dh), lambda h, i: (0, 0)),
                  pl.BlockSpec((L, dh), lambda h, i: (0, h))],
        out_specs=pl.BlockSpec((bq, dh), lambda h, i: (i, h)),
        out_shape=jax.ShapeDtypeStruct((L, D), BF16),
        scratch_shapes=[pltpu.VMEM((bq, dh), F32), pltpu.VMEM((bq, V7X_LANES), F32),
                        pltpu.VMEM((bq, V7X_LANES), F32),
                        pltpu.VMEM((2, bq, bk), F32)],
        compiler_params=_cparams(("arbitrary", "arbitrary"), vmem),
        name="fox_attention",
    )(q, k, caug, v)


def kernel(x, a_norm, a_w_in, a_lambda_re, a_lambda_im, a_log_step, a_b_re, a_b_im, a_c_re, a_c_im, a_d, a_w_glu, kv_norm, w_k, w_v, w_f, b_f, b_norm, b_w_q, b_w_o, ffn_norm, ffn_w_up, ffn_conv_w, ffn_conv_b, ffn_w_down, final_norm):
    bsz, seq, d_model = x.shape
    n_a = a_w_in.shape[0]
    n_b = b_w_q.shape[0]
    n_heads = w_f.shape[1]
    dh = d_model // n_heads
    assert bsz == 1 and n_b >= 1
    xs = x.reshape(seq, d_model).astype(F32)
    d_ff = ffn_w_down.shape[1]
    w_val = _cast_bf16(ffn_w_up, col_range=(d_ff, 2 * d_ff))
    w_down = ffn_w_down
    w_glu = _cast_bf16(a_w_glu)

    def ffn(xs, li, final_g=None):
        return _ffn(xs, li, ffn_norm, ffn_w_up, w_val, ffn_conv_w, ffn_conv_b, w_down,
                    final_g=final_g)

    for i in range(n_a):
        h = _rmsnorm(xs, a_norm[i], BF16, tm=S5_ROW_BLOCK, segment_rows=S5_SEG)
        prep = _s5_prepare(a_lambda_re[i], a_lambda_im[i], a_log_step[i],
                           a_b_re[i], a_b_im[i], a_c_re[i], a_c_im[i], a_d[i])
        y = _s5_core(h, a_w_in, i, *prep)
        xs = _glu_residual(y, w_glu, i, xs)
        xs = ffn(xs, i)

    h_kv = _rmsnorm(xs, kv_norm, BF16)
    k = _matmul(h_kv, w_k, 0, BF16, name="k_proj")
    v = _matmul(h_kv, w_v, 0, BF16, name="v_proj")
    pieces = _cumsum_pieces(_fgate_logf(h_kv, w_f, b_f))
    caug = jnp.concatenate([p[:n_heads] for p in pieces], axis=0).T
    caug = jnp.pad(caug, ((0, 0), (0, dh - CK_PIECES * n_heads)))

    q_scale = dh ** -0.5 * LOG2E
    for j in range(n_b):
        li = n_a + j
        h = _rmsnorm(xs, b_norm[j], BF16)
        q = _matmul(h, b_w_q, j, BF16, out_scale=q_scale, name="q_proj")
        o = _fox_attention(q, k, v, caug, n_heads)
        xs = _matmul(o, b_w_o, j, F32, residual=xs, name="o_proj")
        xs = ffn(xs, li, final_g=final_norm if j == n_b - 1 else None)

    return xs.reshape(bsz, seq, d_model).astype(x.dtype)
```

```python
import functools
import math

import jax
import jax.numpy as jnp
from jax import lax
from jax.experimental import pallas as pl
from jax.experimental.pallas import tpu as pltpu

F32 = jnp.float32
BF16 = jnp.bfloat16

NORM_EPS = 1e-6
CONV_WIDTH = 3
LOG2E = math.log2(math.e)

V7X_SUBLANES = 8
V7X_LANES = 128
V7X_VMEM_LIMIT_BYTES = 58 * 1024 * 1024

S5_GROUPS_PER_BLOCK = 16
S5_ROW_BLOCK = 512
S5_SEG = S5_ROW_BLOCK // V7X_SUBLANES
S5_CHAINS = 2
FFN_ROW_CHUNK = 128
FFN_COL_GROUP = 256
CK_PIECES = 3
ATTN_KEYS_PER_ITER = 4096
MASK_VALUE = -1e30


def _cparams(semantics, vmem_bytes):
    budget = int(vmem_bytes) * 5 // 4 + (4 << 20)
    return pltpu.CompilerParams(
        dimension_semantics=semantics,
        vmem_limit_bytes=min(budget, V7X_VMEM_LIMIT_BYTES))


def _gelu_tanh(x):
    c = math.sqrt(2.0 / math.pi)
    return x * (0.5 * (1.0 + jnp.tanh(c * (x + 0.044715 * (x * x * x)))))


def _rms_normalize(x, g):
    ms = jnp.mean(x * x, axis=-1, keepdims=True)
    return x * lax.rsqrt(ms + NORM_EPS) * g


def _cmul(ar, ai, br, bi):
    return ar * br - ai * bi, ar * bi + ai * br


def _stacked(w):
    return w if w.ndim == 3 else w[None]


def _cast_kernel(w_ref, o_ref):
    o_ref[...] = w_ref[...].astype(o_ref.dtype)


def _cast_bf16(w, col_range=None, tr=512, max_cols=4096):
    shape = w.shape
    c0, c1 = col_range if col_range is not None else (0, shape[-1])
    C = c1 - c0
    R = math.prod(shape[:-1])
    tc = C
    while tc > max_cols:
        tc //= 2
    assert C % tc == 0 and c0 % tc == 0 and tc % V7X_LANES == 0 and R % tr == 0
    out = pl.pallas_call(
        _cast_kernel,
        grid=(R // tr, C // tc),
        in_specs=[pl.BlockSpec((tr, tc), lambda i, j: (i, j + c0 // tc))],
        out_specs=pl.BlockSpec((tr, tc), lambda i, j: (i, j)),
        out_shape=jax.ShapeDtypeStruct((R, C), BF16),
        compiler_params=_cparams(("arbitrary", "arbitrary"), 2 * tr * tc * 6),
        name="cast_bf16",
    )(w.reshape(R, shape[-1]))
    return out.reshape(shape[:-1] + (C,))


def _rmsnorm_kernel(x_ref, g_ref, *rest):
    o_ref = rest[-1]
    h = _rms_normalize(x_ref[...], g_ref[...]).astype(o_ref.dtype)
    if len(rest) == 2:
        h = jnp.dot(rest[0][...], h, preferred_element_type=F32).astype(o_ref.dtype)
    o_ref[...] = h


def _segment_interleave(tm, segment_rows):
    n_seg = tm // segment_rows
    r = jnp.arange(tm)
    src = (r % n_seg) * segment_rows + r // n_seg
    return (src[:, None] == jnp.arange(tm)[None, :]).astype(BF16)


def _rmsnorm(x, g, out_dtype, tm=512, segment_rows=None):
    L, D = x.shape
    in_specs = [pl.BlockSpec((tm, D), lambda i: (i, 0)),
                pl.BlockSpec((1, D), lambda i: (0, 0))]
    args = [x, g.reshape(1, D)]
    if segment_rows is not None:
        assert out_dtype == BF16
        in_specs.append(pl.BlockSpec((tm, tm), lambda i: (0, 0)))
        args.append(_segment_interleave(tm, segment_rows))
    return pl.pallas_call(
        _rmsnorm_kernel,
        grid=(L // tm,),
        in_specs=in_specs,
        out_specs=pl.BlockSpec((tm, D), lambda i: (i, 0)),
        out_shape=jax.ShapeDtypeStruct((L, D), out_dtype),
        compiler_params=_cparams(("arbitrary",), 4 * tm * D * 8),
        name="rmsnorm",
    )(*args)


def _matmul_kernel(a_ref, b_ref, o_ref, *, out_scale):
    acc = jnp.dot(a_ref[...], b_ref[...].astype(BF16), preferred_element_type=F32)
    if out_scale is not None:
        acc = acc * out_scale
    o_ref[...] = acc.astype(o_ref.dtype)


def _matmul_res_kernel(a_ref, b_ref, r_ref, o_ref):
    o_ref[...] = r_ref[...] + jnp.dot(a_ref[...], b_ref[...].astype(BF16),
                                      preferred_element_type=F32)


def _matmul(a, w, layer, out_dtype, residual=None, out_scale=None, tm=2048, tn=512,
            name="matmul"):
    L, K = a.shape
    w = _stacked(w)
    N = w.shape[2]
    in_specs = [pl.BlockSpec((tm, K), lambda i, j: (i, 0)),
                pl.BlockSpec((None, K, tn), lambda i, j: (layer, 0, j))]
    args = [a, w]
    if residual is not None:
        in_specs.append(pl.BlockSpec((tm, tn), lambda i, j: (i, j)))
        args.append(residual)
        kern = _matmul_res_kernel
    else:
        kern = functools.partial(_matmul_kernel, out_scale=out_scale)
    vmem = 2 * (tm * K * 2 + K * tn * 4 + 2 * tm * tn * 4) + K * tn * 2 + 2 * tm * tn * 4
    return pl.pallas_call(
        kern,
        grid=(L // tm, N // tn),
        in_specs=in_specs,
        out_specs=pl.BlockSpec((tm, tn), lambda i, j: (i, j)),
        out_shape=jax.ShapeDtypeStruct((L, N), out_dtype),
        compiler_params=_cparams(("arbitrary", "arbitrary"), vmem),
        name=name,
    )(*args)


def _cpowers(br, bi, n):
    pr, pi = br[None], bi[None]
    k = 1
    while k < n:
        tr, ti = _cmul(pr, pi, pr[k - 1:k], pi[k - 1:k])
        pr = jnp.concatenate([pr, tr], axis=0)
        pi = jnp.concatenate([pi, ti], axis=0)
        k *= 2
    return pr, pi


def _s5_prepare(lam_re, lam_im, log_step, b_re, b_im, c_re, c_im, d):
    G, P = lam_re.shape
    C = b_re.shape[-1]
    gb = S5_GROUPS_PER_BLOCK
    nb = G // gb
    S = V7X_SUBLANES
    lam_re = lam_re.astype(F32)
    lam_im = lam_im.astype(F32)
    dt = jnp.exp(log_step.astype(F32))[:, None]
    mag = jnp.exp(lam_re * dt)
    lbr = mag * jnp.cos(lam_im * dt)
    lbi = mag * jnp.sin(lam_im * dt)
    den = lam_re * lam_re + lam_im * lam_im
    fr = ((lbr - 1.0) * lam_re + lbi * lam_im) / den
    fi = (lbi * lam_re - (lbr - 1.0) * lam_im) / den
    bbr, bbi = _cmul(fr[..., None], fi[..., None], b_re.astype(F32), b_im.astype(F32))
    def block_diag(m):
        a, b = m.shape[1:]
        mt = m.reshape(nb, gb, a, b).transpose(0, 1, 3, 2).reshape(nb, gb * b, a)
        same = ((jnp.arange(gb * b) // b)[:, None]
                == (jnp.arange(gb * a) // a)[None, :])
        return jnp.where(same, jnp.tile(mt, (1, 1, gb)), 0.0)

    wb = jnp.concatenate([block_diag(bbr), block_diag(bbi)], axis=-1).astype(BF16)
    wc = jnp.concatenate([block_diag(c_re.astype(F32)), block_diag(-c_im.astype(F32))],
                         axis=-2).astype(BF16)

    def lanes(m):
        return m.reshape(m.shape[0], nb, gb * P).transpose(1, 0, 2)

    tr, ti = _cpowers(lbr, lbi, S5_SEG)
    tab = jnp.concatenate([lanes(tr), lanes(ti)], axis=-1)
    sr, si = _cpowers(tr[-1], ti[-1], S)
    rows = jnp.arange(S)[:, None, None]
    kinds = [jnp.broadcast_to(lbr[None], (S, G, P)),
             jnp.broadcast_to(lbi[None], (S, G, P))]
    for k in (1, 2, 4):
        kinds += [jnp.where(rows >= k, sr[k - 1][None], 0.0),
                  jnp.where(rows >= k, si[k - 1][None], 0.0)]
    kinds += [sr, si]
    cst = jnp.stack([lanes(m) for m in kinds], axis=1)
    dvec = d.astype(F32).reshape(nb, 1, gb * C)
    return wb, wc, cst, tab, dvec


def _s5_unpermutation(tm):
    l = jnp.arange(tm)
    src = (l % S5_SEG) * V7X_SUBLANES + l // S5_SEG
    return (src[:, None] == jnp.arange(tm)[None, :]).astype(BF16)


def _s5_kernel(h_ref, win_ref, wb_ref, wc_ref, cst_ref, tab_ref, d_ref,
               unperm_ref, o_ref, st_ref, carry_ref, *, tm, ns, cb):
    S = V7X_SUBLANES
    n_tiles = tm // S

    @pl.when(pl.program_id(1) == 0)
    def _():
        carry_ref[...] = jnp.zeros_like(carry_ref)

    h = h_ref[...]

    def project(c):
        u = jnp.dot(h, win_ref[:, c * cb:(c + 1) * cb].astype(BF16),
                    preferred_element_type=F32)
        st_ref[c] = jnp.dot(u.astype(BF16), wb_ref[c],
                            preferred_element_type=F32)
        return u

    def scan(c):
        ar = cst_ref[c, 0]
        ai = cst_ref[c, 1]
        dr = jnp.zeros((S, ns), F32)
        di = jnp.zeros((S, ns), F32)
        for t in range(n_tiles):
            rows = slice(t * S, (t + 1) * S)
            dr, di = (ar * dr - ai * di + st_ref[c, rows, 0:ns],
                      ar * di + ai * dr + st_ref[c, rows, ns:2 * ns])
            st_ref[c, rows, 0:ns] = dr
            st_ref[c, rows, ns:2 * ns] = di
        for lvl, k in enumerate((1, 2, 4)):
            br = cst_ref[c, 2 + 2 * lvl]
            bi = cst_ref[c, 3 + 2 * lvl]
            sr = pltpu.roll(dr, k, 0)
            si = pltpu.roll(di, k, 0)
            dr, di = dr + br * sr - bi * si, di + br * si + bi * sr
        cin_r = carry_ref[c, 0:1, :]
        cin_i = carry_ref[c, 1:2, :]
        pr = cst_ref[c, 8]
        pi = cst_ref[c, 9]
        dr, di = dr + pr * cin_r - pi * cin_i, di + pr * cin_i + pi * cin_r
        first = lax.broadcasted_iota(jnp.int32, (S, ns), 0) == 0
        cr = jnp.where(first, cin_r, pltpu.roll(dr, 1, 0))
        ci = jnp.where(first, cin_i, pltpu.roll(di, 1, 0))
        carry_ref[c, 0:1, :] = dr[S - 1:S, :]
        carry_ref[c, 1:2, :] = di[S - 1:S, :]
        for t in range(n_tiles):
            rows = slice(t * S, (t + 1) * S)
            tr = tab_ref[c, t:t + 1, 0:ns]
            ti = tab_ref[c, t:t + 1, ns:2 * ns]
            st_ref[c, rows, 0:ns] += tr * cr - ti * ci
            st_ref[c, rows, ns:2 * ns] += tr * ci + ti * cr

    def emit(c, u):
        y = jnp.dot(st_ref[c].astype(BF16), wc_ref[c],
                    preferred_element_type=F32) + d_ref[c] * u
        act = _gelu_tanh(y).astype(BF16)
        o_ref[:, c * cb:(c + 1) * cb] = jnp.dot(
            unperm_ref[...], act, preferred_element_type=F32).astype(o_ref.dtype)

    us = [project(c) for c in range(S5_CHAINS)]
    for c in range(S5_CHAINS):
        scan(c)
        emit(c, us[c])


def _s5_core(h, w_in, layer, wb, wc, cst, tab, dvec):
    L, D = h.shape
    tm = S5_ROW_BLOCK
    nc = S5_CHAINS
    nb, cb, ns2 = wb.shape
    ns = ns2 // 2
    vmem = (2 * (tm * D * 2 + nc * (D * cb * 4 + 2 * cb * ns2 * 2 + 10 * 8 * ns * 4
                                    + S5_SEG * ns2 * 4 + tm * cb * 2) + tm * tm * 2)
            + nc * (tm * ns2 * 4 + D * cb * 2 + 4 * tm * cb * 4) + tm * ns2 * 4)
    return pl.pallas_call(
        functools.partial(_s5_kernel, tm=tm, ns=ns, cb=cb),
        grid=(nb // nc, L // tm),
        in_specs=[
            pl.BlockSpec((tm, D), lambda j, i: (i, 0)),
            pl.BlockSpec((None, D, nc * cb), lambda j, i: (layer, 0, j)),
            pl.BlockSpec((nc, cb, ns2), lambda j, i: (j, 0, 0)),
            pl.BlockSpec((nc, ns2, cb), lambda j, i: (j, 0, 0)),
            pl.BlockSpec((nc, 10, V7X_SUBLANES, ns), lambda j, i: (j, 0, 0, 0)),
            pl.BlockSpec((nc, S5_SEG, ns2), lambda j, i: (j, 0, 0)),
            pl.BlockSpec((nc, 1, cb), lambda j, i: (j, 0, 0)),
            pl.BlockSpec((tm, tm), lambda j, i: (0, 0)),
        ],
        out_specs=pl.BlockSpec((tm, nc * cb), lambda j, i: (i, j)),
        out_shape=jax.ShapeDtypeStruct((L, D), BF16),
        scratch_shapes=[pltpu.VMEM((nc, tm, ns2), F32),
                        pltpu.VMEM((nc, V7X_SUBLANES, ns), F32)],
        compiler_params=_cparams(("arbitrary", "arbitrary"), vmem),
        name="s5_core",
    )(h, w_in, wb, wc, cst, tab, dvec, _s5_unpermutation(tm))


def _glu_kernel(y_ref, wv_ref, wg_ref, x_ref, o_ref):
    y = y_ref[...]
    val = jnp.dot(y, wv_ref[...], preferred_element_type=F32)
    gate = jnp.dot(y, wg_ref[...], preferred_element_type=F32)
    o_ref[...] = x_ref[...] + val * (1.0 / (1.0 + jnp.exp(-gate)))


def _glu_residual(y, w_glu, layer, x, tm=2048, tn=256):
    L, D = y.shape
    nj = D // tn
    vmem = 2 * (tm * D * 2 + 2 * D * tn * 2 + 2 * tm * tn * 4) + 4 * tm * tn * 4
    return pl.pallas_call(
        _glu_kernel,
        grid=(L // tm, nj),
        in_specs=[pl.BlockSpec((tm, D), lambda i, j: (i, 0)),
                  pl.BlockSpec((None, D, tn), lambda i, j: (layer, 0, j)),
                  pl.BlockSpec((None, D, tn), lambda i, j: (layer, 0, nj + j)),
                  pl.BlockSpec((tm, tn), lambda i, j: (i, j))],
        out_specs=pl.BlockSpec((tm, tn), lambda i, j: (i, j)),
        out_shape=jax.ShapeDtypeStruct((L, D), F32),
        compiler_params=_cparams(("arbitrary", "arbitrary"), vmem),
        name="s5_glu",
    )(y, w_glu, w_glu, x)


def _ffn_kernel(x_ref, g_ref, wg_ref, wv_ref, cw_ref, cb_ref, wd_ref, *rest,
                tm, tf, final_norm):
    if final_norm:
        fg_ref, o_ref, h_ref, halo_ref, gs_ref = rest
    else:
        o_ref, h_ref, halo_ref, gs_ref = rest
    i = pl.program_id(0)
    f = pl.program_id(1)
    H = V7X_SUBLANES

    def row_chunks(fn):
        def body(c, carry):
            fn(pl.ds(pl.multiple_of(c * FFN_ROW_CHUNK, FFN_ROW_CHUNK), FFN_ROW_CHUNK))
            return carry
        lax.fori_loop(0, tm // FFN_ROW_CHUNK, body, 0)

    @pl.when(f == 0)
    def _():
        def normalize(rows):
            h_ref[rows, :] = _rms_normalize(x_ref[rows, :], g_ref[...]).astype(BF16)
            o_ref[rows, :] = jnp.zeros((FFN_ROW_CHUNK, o_ref.shape[1]), F32)
        row_chunks(normalize)

    h = h_ref[...]

    def up(cols):
        gate = jnp.dot(h, wg_ref[:, cols].astype(BF16), preferred_element_type=F32)
        val = jnp.dot(h, wv_ref[:, cols], preferred_element_type=F32)
        return gate, val

    def activate(cols, gate, val):
        prev = halo_ref[f, :, cols]
        gs_ref[0:H, cols] = jnp.where(i == 0, jnp.zeros_like(prev), prev)
        gs_ref[H:H + tm, cols] = gate
        halo_ref[f, :, cols] = gate[tm - H:, :]
        g1 = gs_ref[H - 1:H - 1 + tm, cols]
        g2 = gs_ref[H - 2:H - 2 + tm, cols]
        conv = (cw_ref[0:1, cols] * g2 + cw_ref[1:2, cols] * g1
                + cw_ref[2:3, cols] * gate + cb_ref[:, cols])
        return (_gelu_tanh(conv) * val).astype(BF16)

    groups = [slice(c, c + FFN_COL_GROUP) for c in range(0, tf, FFN_COL_GROUP)]
    ups = [up(cols) for cols in groups]
    for cols, (gate, val) in zip(groups, ups):
        act = activate(cols, gate, val)
        o_ref[...] += jnp.dot(act, wd_ref[cols, :].astype(BF16),
                              preferred_element_type=F32)

    @pl.when(f == pl.num_programs(1) - 1)
    def _():
        def finish(rows):
            y = x_ref[rows, :] + o_ref[rows, :]
            if final_norm:
                y = _rms_normalize(y, fg_ref[...])
            o_ref[rows, :] = y
        row_chunks(finish)


def _ffn(x, layer, ffn_norm, w_up, w_val, conv_w, conv_b, w_down, final_g=None,
         tm=1024, tf=512):
    L, D = x.shape
    n_layers, F, _ = w_down.shape
    nf = F // tf
    final_norm = final_g is not None
    in_specs = [
        pl.BlockSpec((tm, D), lambda i, f: (i, 0), pipeline_mode=pl.Buffered(1)),
        pl.BlockSpec((None, 1, D), lambda i, f: (layer, 0, 0)),
        pl.BlockSpec((None, D, tf), lambda i, f: (layer, 0, f)),
        pl.BlockSpec((None, D, tf), lambda i, f: (layer, 0, f)),
        pl.BlockSpec((None, CONV_WIDTH, tf), lambda i, f: (layer, 0, f)),
        pl.BlockSpec((None, 1, tf), lambda i, f: (layer, 0, f)),
        pl.BlockSpec((None, tf, D), lambda i, f: (layer, f, 0)),
    ]
    args = [x, ffn_norm.reshape(n_layers, 1, D), w_up, w_val, conv_w,
            conv_b.reshape(n_layers, 1, F), w_down]
    if final_norm:
        in_specs.append(pl.BlockSpec((1, D), lambda i, f: (0, 0)))
        args.append(final_g.reshape(1, D))
    vmem = (tm * D * 4 + 2 * tm * D * 4 + 2 * 5 * D * tf * 2 + 2 * D * tf * 2 + tm * D * 2
            + nf * V7X_SUBLANES * tf * 4 + (tm + 8) * tf * 4 + 6 * tm * tf * 4
            + tm * D * 4)
    return pl.pallas_call(
        functools.partial(_ffn_kernel, tm=tm, tf=tf, final_norm=final_norm),
        grid=(L // tm, nf),
        in_specs=in_specs,
        out_specs=pl.BlockSpec((tm, D), lambda i, f: (i, 0)),
        out_shape=jax.ShapeDtypeStruct((L, D), F32),
        scratch_shapes=[pltpu.VMEM((tm, D), BF16),
                        pltpu.VMEM((nf, V7X_SUBLANES, tf), F32),
                        pltpu.VMEM((tm + V7X_SUBLANES, tf), F32)],
        compiler_params=_cparams(("arbitrary", "arbitrary"), vmem),
        name="conv_glu_ffn",
    )(*args)


def _fgate_kernel(wft_ref, h_ref, bf_ref, o_ref):
    z = lax.dot_general(wft_ref[...], h_ref[...], (((1,), (1,)), ((), ())),
                        preferred_element_type=F32)
    z = z + bf_ref[...]
    o_ref[...] = jnp.minimum(z, 0.0) - jnp.log1p(jnp.exp(-jnp.abs(z)))


def _fgate_logf(h_kv, w_f, b_f, tm=1024):
    L, D = h_kv.shape
    H = w_f.shape[1]
    hp = V7X_LANES
    wft = jnp.zeros((hp, D), BF16).at[:H].set(w_f.T.astype(BF16))
    bfp = jnp.zeros((hp, 1), F32).at[:H, 0].set(b_f.astype(F32))
    return pl.pallas_call(
        _fgate_kernel,
        grid=(L // tm,),
        in_specs=[pl.BlockSpec((hp, D), lambda i: (0, 0)),
                  pl.BlockSpec((tm, D), lambda i: (i, 0)),
                  pl.BlockSpec((hp, 1), lambda i: (0, 0))],
        out_specs=pl.BlockSpec((hp, tm), lambda i: (0, i)),
        out_shape=jax.ShapeDtypeStruct((hp, L), F32),
        compiler_params=_cparams(("arbitrary",), 2 * (tm * D * 2 + hp * D * 2)
                                 + 8 * hp * tm * 4),
        name="fgate_logf",
    )(wft, h_kv, bfp)


def _cumsum_kernel(x_ref, *o_refs, tb):
    r = lax.broadcasted_iota(jnp.int32, (tb, tb), 0)
    c = lax.broadcasted_iota(jnp.int32, (tb, tb), 1)
    tri = (r <= c).astype(F32)

    def body(b, carry):
        off = pl.multiple_of(b * tb, tb)
        cs = jnp.dot(x_ref[:, pl.ds(off, tb)], tri,
                     precision=lax.Precision.HIGHEST,
                     preferred_element_type=F32) + carry
        rem = cs * (-LOG2E)
        for o_ref in o_refs:
            piece = rem.astype(BF16)
            o_ref[:, pl.ds(off, tb)] = piece
            rem = rem - piece.astype(F32)
        return cs[:, tb - 1:tb]

    lax.fori_loop(0, x_ref.shape[1] // tb, body,
                  jnp.zeros((x_ref.shape[0], 1), F32))


def _cumsum_pieces(x, tb=256):
    R, L = x.shape
    return pl.pallas_call(
        functools.partial(_cumsum_kernel, tb=tb),
        out_shape=[jax.ShapeDtypeStruct((R, L), BF16)] * CK_PIECES,
        compiler_params=pltpu.CompilerParams(
            vmem_limit_bytes=min(8 * R * L * 4, V7X_VMEM_LIMIT_BYTES)),
        name="logf_cumsum",
    )(x)


def _attn_kernel(q_ref, k_ref, c_ref, v_ref, o_ref, acc_ref, m_ref, l_ref, s_ref,
                 *, bq, bk, dh, n_heads):
    head = pl.program_id(0)
    qi = pl.program_id(1)
    W = V7X_LANES
    n_lane_tiles = bk // W
    lane = lax.broadcasted_iota(jnp.int32, (bq, dh), 1)
    pick = (lane % n_heads == head) & (lane < CK_PIECES * n_heads)
    qa = jnp.concatenate([q_ref[...], pick.astype(BF16)], axis=1)
    acc_ref[...] = jnp.zeros_like(acc_ref)
    m_ref[...] = jnp.full_like(m_ref, MASK_VALUE)
    l_ref[...] = jnp.zeros_like(l_ref)

    def scores(kj):
        off = pl.multiple_of(kj * bk, bk)
        ka = jnp.concatenate([k_ref[pl.ds(off, bk), :], c_ref[pl.ds(off, bk), :]],
                             axis=1)
        return lax.dot_general(qa, ka, (((1,), (1,)), ((), ())),
                               preferred_element_type=F32)

    s_ref[0] = scores(0)

    def step(kj, slot, masked, last):
        off = pl.multiple_of(kj * bk, bk)
        if not last:
            s_ref[1 - slot] = scores(kj + 1)
        tiles = []
        for t in range(n_lane_tiles):
            s = s_ref[slot, :, t * W:(t + 1) * W]
            if masked:
                key = off + t * W + lax.broadcasted_iota(jnp.int32, (bq, W), 1)
                qry = qi * bq + lax.broadcasted_iota(jnp.int32, (bq, W), 0)
                s = jnp.where(key <= qry, s, MASK_VALUE)
            tiles.append(s)
        tile_max = functools.reduce(jnp.maximum, tiles)
        row_max = jnp.broadcast_to(jnp.max(tile_max, axis=1, keepdims=True), (bq, W))
        m_old = m_ref[...]
        m_new = jnp.maximum(m_old, row_max)
        alpha = jnp.exp2(m_old - m_new)
        m_ref[...] = m_new
        p = [jnp.exp2(s - m_new) for s in tiles]
        l_ref[...] = alpha * l_ref[...] + functools.reduce(lambda a, b: a + b, p)
        pv = jnp.dot(jnp.concatenate([x.astype(BF16) for x in p], axis=1),
                     v_ref[pl.ds(off, bk), :], preferred_element_type=F32)
        acc_ref[...] = alpha * acc_ref[...] + pv

    n_diag = bq // bk
    unroll = max(2, ATTN_KEYS_PER_ITER // bk)
    assert dh == W and unroll % 2 == 0 and unroll % n_diag == 0

    def full_steps(kj0, n):
        for d in range(n):
            step(kj0 + d, d % 2, False, False)

    def body(jj, carry):
        full_steps(unroll * jj, unroll)
        return carry

    n_full = qi * n_diag
    lax.fori_loop(0, n_full // unroll, body, 0)
    rem = n_full % unroll
    for r in range(0, unroll, n_diag):
        @pl.when(rem == r)
        def _(r=r):
            full_steps(n_full - r, r)
            for dj in range(n_diag):
                step(n_full + dj, (r + dj) % 2, True, dj == n_diag - 1)

    l = jnp.sum(l_ref[...], axis=1, keepdims=True)
    o_ref[...] = (acc_ref[...] / l).astype(o_ref.dtype)


def _fox_attention(q, k, v, caug, n_heads, bq=1024, bk=1024):
    L, D = q.shape
    dh = D // n_heads
    assert CK_PIECES * n_heads <= dh
    vmem = (2 * (3 * L * dh * 2 + 2 * bq * dh * 2) + 3 * bq * dh * 4 + 2 * bq * bk * 4
            + 4 * bq * bk * 4)
    return pl.pallas_call(
        functools.partial(_attn_kernel, bq=bq, bk=bk, dh=dh, n_heads=n_heads),
        grid=(n_heads, L // bq),
        in_specs=[pl.BlockSpec((bq, dh), lambda h, i: (i, h)),
                  pl.BlockSpec((L, dh), lambda h, i: (0, h)),
                  pl.BlockSpec((L, dh), lambda h, i: (0, 0)),
                  pl.BlockSpec((L, dh), lambda h, i: (0, h))],
        out_specs=pl.BlockSpec((bq, dh), lambda h, i: (i, h)),
        out_shape=jax.ShapeDtypeStruct((L, D), BF16),
        scratch_shapes=[pltpu.VMEM((bq, dh), F32), pltpu.VMEM((bq, V7X_LANES), F32),
                        pltpu.VMEM((bq, V7X_LANES), F32),
                        pltpu.VMEM((2, bq, bk), F32)],
        compiler_params=_cparams(("arbitrary", "arbitrary"), vmem),
        name="fox_attention",
    )(q, k, caug, v)


def kernel(x, a_norm, a_w_in, a_lambda_re, a_lambda_im, a_log_step, a_b_re, a_b_im, a_c_re, a_c_im, a_d, a_w_glu, kv_norm, w_k, w_v, w_f, b_f, b_norm, b_w_q, b_w_o, ffn_norm, ffn_w_up, ffn_conv_w, ffn_conv_b, ffn_w_down, final_norm):
    bsz, seq, d_model = x.shape
    n_a = a_w_in.shape[0]
    n_b = b_w_q.shape[0]
    n_heads = w_f.shape[1]
    dh = d_model // n_heads
    assert bsz == 1 and n_b >= 1
    xs = x.reshape(seq, d_model).astype(F32)
    d_ff = ffn_w_down.shape[1]
    w_val = _cast_bf16(ffn_w_up, col_range=(d_ff, 2 * d_ff))
    w_down = ffn_w_down
    w_glu = _cast_bf16(a_w_glu)

    def ffn(xs, li, final_g=None):
        return _ffn(xs, li, ffn_norm, ffn_w_up, w_val, ffn_conv_w, ffn_conv_b, w_down,
                    final_g=final_g)

    for i in range(n_a):
        h = _rmsnorm(xs, a_norm[i], BF16, tm=S5_ROW_BLOCK, segment_rows=S5_SEG)
        prep = _s5_prepare(a_lambda_re[i], a_lambda_im[i], a_log_step[i],
                           a_b_re[i], a_b_im[i], a_c_re[i], a_c_im[i], a_d[i])
        y = _s5_core(h, a_w_in, i, *prep)
        xs = _glu_residual(y, w_glu, i, xs)
        xs = ffn(xs, i)

    h_kv = _rmsnorm(xs, kv_norm, BF16)
    k = _matmul(h_kv, w_k, 0, BF16, name="k_proj")
    v = _matmul(h_kv, w_v, 0, BF16, name="v_proj")
    pieces = _cumsum_pieces(_fgate_logf(h_kv, w_f, b_f))
    caug = jnp.concatenate([p[:n_heads] for p in pieces], axis=0).T
    caug = jnp.pad(caug, ((0, 0), (0, dh - CK_PIECES * n_heads)))

    q_scale = dh ** -0.5 * LOG2E
    for j in range(n_b):
        li = n_a + j
        h = _rmsnorm(xs, b_norm[j], BF16)
        q = _matmul(h, b_w_q, j, BF16, out_scale=q_scale, name="q_proj")
        o = _fox_attention(q, k, v, caug, n_heads)
        xs = _matmul(o, b_w_o, j, F32, residual=xs, name="o_proj")
        xs = ffn(xs, li, final_g=final_norm if j == n_b - 1 else None)

    return xs.reshape(bsz, seq, d_model).astype(x.dtype)
```
